```python
import jax, jax.numpy as jnp
from jax import lax
import numpy as np

D_MODEL = 1024
BATCH = 16
SEQ = 2048
DEPTH = 1

CHUNK = 64
HEAD_DIM = 64
A_HEADS = 8
A_PREV_CHUNKS = 8
A_MAX_REL = 128
B_Q_HEADS = 8
B_KV_HEADS = 2
B_GROUP = B_Q_HEADS // B_KV_HEADS
B_WINDOW = 128
B_PREV_CHUNKS = (B_WINDOW - 1 + CHUNK - 1) // CHUNK
A_WIDTH = A_HEADS * HEAD_DIM
B_Q_WIDTH = B_Q_HEADS * HEAD_DIM
B_KV_WIDTH = B_KV_HEADS * HEAD_DIM
IN_COLS = 3 * A_WIDTH + B_Q_WIDTH + 2 * B_KV_WIDTH
D_FF = 2816
PLE_DIM = 256
EPS = 1e-6
NEG_INF = -1e30

kernel_name = "hybrid_chunked_relpos_swa_sink_macaron_ple"


def rms_norm(x, gain):
    xf = x.astype(jnp.float32)
    y = xf * lax.rsqrt(jnp.mean(xf * xf, axis=-1, keepdims=True) + EPS)
    return (y * gain.astype(jnp.float32)).astype(x.dtype)


def swiglu_ffn(x, w_gu, w_down):
    g, u = jnp.split(x @ w_gu, 2, axis=-1)
    return (jax.nn.silu(g) * u) @ w_down


def alibi_slopes(n_heads):
    return np.array([2.0 ** (-8.0 * (h + 1) / n_heads) for h in range(n_heads)], dtype=np.float32)


def band_distance(n_prev):
    i = np.arange(CHUNK)[:, None]
    j = np.arange((n_prev + 1) * CHUNK)[None, :]
    return i + n_prev * CHUNK - j


def chunk_band_attention(q, k, v, n_prev, bias, sink):
    b, hkv, g, s, dh = q.shape
    n_chunks = s // CHUNK
    band = (n_prev + 1) * CHUNK
    pad = n_prev * CHUNK
    kp = jnp.pad(k, ((0, 0), (0, 0), (pad, 0), (0, 0)))
    vp = jnp.pad(v, ((0, 0), (0, 0), (pad, 0), (0, 0)))
    scale = dh ** -0.5
    key_idx = jnp.arange(band)

    def one_chunk(c):
        start = c * CHUNK
        qc = lax.dynamic_slice_in_dim(q, start, CHUNK, axis=3)
        kc = lax.dynamic_slice_in_dim(kp, start, band, axis=2)
        vc = lax.dynamic_slice_in_dim(vp, start, band, axis=2)
        scores = jnp.einsum('bkgqd,bksd->bkgqs', qc.astype(jnp.float32),
                            kc.astype(jnp.float32)) * scale + bias
        valid = key_idx >= (n_prev - c) * CHUNK
        scores = jnp.where(valid, scores, NEG_INF)
        if sink is None:
            probs = jax.nn.softmax(scores, axis=-1)
        else:
            sink_col = jnp.broadcast_to(sink.astype(jnp.float32).reshape(1, hkv, g, 1, 1),
                                        (b, hkv, g, CHUNK, 1))
            probs = jax.nn.softmax(jnp.concatenate([scores, sink_col], axis=-1), axis=-1)[..., :band]
        out = jnp.einsum('bkgqs,bksd->bkgqd', probs, vc.astype(jnp.float32))
        return out.astype(v.dtype)

    outs = lax.map(one_chunk, jnp.arange(n_chunks))
    outs = jnp.transpose(outs, (1, 0, 4, 2, 3, 5))
    return outs.reshape(b, s, hkv * g * dh)


def setup_inputs(seed: int = 0) -> dict:
    key = jax.random.key(seed)
    ks = jax.random.split(key, 24)
    f32 = jnp.float32

    def w(k, shape, fan_in):
        return jax.random.normal(k, shape, f32) * (fan_in ** -0.5)

    def gain(k, n):
        return 1.0 + 0.05 * jax.random.normal(k, (DEPTH, n), f32)

    return {
        "x": jax.random.normal(ks[0], (BATCH, SEQ, D_MODEL), f32),
        "p": jax.random.normal(ks[1], (DEPTH, BATCH, SEQ, PLE_DIM), f32),
        "ffn1_norm": gain(ks[2], D_MODEL),
        "ffn1_w_gu": w(ks[3], (DEPTH, D_MODEL, 2 * D_FF), D_MODEL),
        "ffn1_w_down": w(ks[4], (DEPTH, D_FF, D_MODEL), D_FF),
        "mix_norm": gain(ks[5], D_MODEL),
        "w_in": w(ks[6], (DEPTH, D_MODEL, IN_COLS), D_MODEL),
        "a_q_norm": gain(ks[7], HEAD_DIM),
        "a_k_norm": gain(ks[8], HEAD_DIM),
        "a_rel_bias": 0.1 * jax.random.normal(ks[9], (DEPTH, A_HEADS, 2 * A_MAX_REL + 1), f32),
        "b_q_norm": gain(ks[10], HEAD_DIM),
        "b_k_norm": gain(ks[11], HEAD_DIM),
        "b_sinks": 0.5 * jax.random.normal(ks[12], (DEPTH, B_Q_HEADS), f32),
        "w_gate": w(ks[13], (DEPTH, D_MODEL, 2 * D_MODEL), D_MODEL),
        "w_proj_a": w(ks[14], (DEPTH, A_WIDTH, D_MODEL), A_WIDTH),
        "w_proj_b": w(ks[15], (DEPTH, B_Q_WIDTH, D_MODEL), B_Q_WIDTH),
        "w_out": w(ks[16], (DEPTH, D_MODEL, D_MODEL), D_MODEL),
        "ffn2_norm": gain(ks[17], D_MODEL),
        "ffn2_w_gu": w(ks[18], (DEPTH, D_MODEL, 2 * D_FF), D_MODEL),
        "ffn2_w_down": w(ks[19], (DEPTH, D_FF, D_MODEL), D_FF),
        "ple_norm": gain(ks[20], D_MODEL),
        "w_ple_gate": w(ks[21], (DEPTH, D_MODEL, D_MODEL), D_MODEL),
        "w_ple_proj": w(ks[22], (DEPTH, PLE_DIM, D_MODEL), PLE_DIM),
    }


def reference(x, p, ffn1_norm, ffn1_w_gu, ffn1_w_down, mix_norm, w_in, a_q_norm, a_k_norm,
              a_rel_bias, b_q_norm, b_k_norm, b_sinks, w_gate, w_proj_a, w_proj_b, w_out,
              ffn2_norm, ffn2_w_gu, ffn2_w_down, ple_norm, w_ple_gate, w_ple_proj):
    b, s, _ = x.shape
    split_points = list(np.cumsum([A_WIDTH, A_WIDTH, A_WIDTH, B_Q_WIDTH, B_KV_WIDTH]))
    a_rel_idx = np.clip(band_distance(A_PREV_CHUNKS), -A_MAX_REL, A_MAX_REL) + A_MAX_REL
    b_dist = np.abs(band_distance(B_PREV_CHUNKS)).astype(np.float32)
    b_alibi = jnp.asarray((-alibi_slopes(B_Q_HEADS)[:, None, None] * b_dist[None])
                          .reshape(B_KV_HEADS, B_GROUP, CHUNK, -1))

    h = x
    for i in range(DEPTH):
        h = h + 0.5 * swiglu_ffn(rms_norm(h, ffn1_norm[i]), ffn1_w_gu[i], ffn1_w_down[i])

        u = rms_norm(h, mix_norm[i])
        qa, ka, va, qb, kb, vb = jnp.split(u @ w_in[i], split_points, axis=-1)

        qa = rms_norm(qa.reshape(b, s, A_HEADS, HEAD_DIM), a_q_norm[i])
        ka = rms_norm(ka.reshape(b, s, A_HEADS, HEAD_DIM), a_k_norm[i])
        qa = jnp.transpose(qa, (0, 2, 1, 3))[:, :, None]
        ka = jnp.transpose(ka, (0, 2, 1, 3))
        va = jnp.transpose(va.reshape(b, s, A_HEADS, HEAD_DIM), (0, 2, 1, 3))
        a_bias = a_rel_bias[i].astype(jnp.float32)[:, a_rel_idx][:, None]
        ya = chunk_band_attention(qa, ka, va, A_PREV_CHUNKS, a_bias, None)

        qb = rms_norm(qb.reshape(b, s, B_KV_HEADS, B_GROUP, HEAD_DIM), b_q_norm[i])
        kb = rms_norm(kb.reshape(b, s, B_KV_HEADS, HEAD_DIM), b_k_norm[i])
        qb = jnp.transpose(qb, (0, 2, 3, 1, 4))
        kb = jnp.transpose(kb, (0, 2, 1, 3))
        vb = jnp.transpose(vb.reshape(b, s, B_KV_HEADS, HEAD_DIM), (0, 2, 1, 3))
        yb = chunk_band_attention(qb, kb, vb, B_PREV_CHUNKS, b_alibi,
                                  b_sinks[i].reshape(B_KV_HEADS, B_GROUP))

        ga, gb = jnp.split(jax.nn.sigmoid(u @ w_gate[i]), 2, axis=-1)
        merged = ga * (ya @ w_proj_a[i]) + gb * (yb @ w_proj_b[i])
        h = h + merged @ w_out[i]

        h = h + 0.5 * swiglu_ffn(rms_norm(h, ffn2_norm[i]), ffn2_w_gu[i], ffn2_w_down[i])

        ple_gate = jax.nn.sigmoid(rms_norm(h, ple_norm[i]) @ w_ple_gate[i])
        h = h + ple_gate * (p[i] @ w_ple_proj[i])
    return h
```

```python
import functools

import numpy as np
import jax
import jax.numpy as jnp
from jax import lax
from jax.experimental import pallas as pl
from jax.experimental.pallas import tpu as pltpu

D_MODEL = 1024
D_FF = 2816
PLE_DIM = 256
CHUNK = 64
HEAD_DIM = 64
A_HEADS = 8
A_PREV_CHUNKS = 8
A_MAX_REL = 128
B_Q_HEADS = 8
B_KV_HEADS = 2
B_GROUP = B_Q_HEADS // B_KV_HEADS
B_PREV_CHUNKS = 2
A_WIDTH = A_HEADS * HEAD_DIM
B_Q_WIDTH = B_Q_HEADS * HEAD_DIM
B_KV_WIDTH = B_KV_HEADS * HEAD_DIM
IN_COLS = 3 * A_WIDTH + B_Q_WIDTH + 2 * B_KV_WIDTH
EPS = 1e-6
NEG_INF = -1e30

V7X_MXU_DIM = 256
V7X_LANES = 128
V7X_VMEM_BYTES = 64 * 1024 * 1024
VMEM_LIMIT_BYTES = V7X_VMEM_BYTES - 8 * 1024 * 1024

ROW_TILE = 512
FF_CHUNK = V7X_MXU_DIM
Q_TILE = 2 * CHUNK
NORM_SLAB = V7X_MXU_DIM

BF16 = jnp.bfloat16
F32 = jnp.float32


def _dot(a, b):
    return jnp.dot(a, b, preferred_element_type=F32)


def _sigmoid(x):
    return 1.0 / (1.0 + jnp.exp(-x))


def _rms_norm(h, gain):
    ms = jnp.mean(h * h, axis=-1, keepdims=True)
    return (h * lax.rsqrt(ms + EPS)) * gain


def _swiglu(xn, wgu_ref, wd_ref, act_ref):
    for j in range(D_FF // FF_CHUNK):
        lo = j * FF_CHUNK
        g = _dot(xn, wgu_ref[:, lo:lo + FF_CHUNK])
        u = _dot(xn, wgu_ref[:, D_FF + lo:D_FF + lo + FF_CHUNK])
        act_ref[:, lo:lo + FF_CHUNK] = ((g * _sigmoid(g)) * u).astype(BF16)
    return _dot(act_ref[...], wd_ref[...])


def _head_group_norm(x, pool, gain):
    ms = _dot((x * x).astype(BF16), pool)
    return (x * lax.rsqrt(ms + EPS)) * gain


def _proj_slabs():
    slabs = []
    col = 0
    for width, normed in ((A_WIDTH, True), (A_WIDTH, True), (A_WIDTH, False),
                          (B_Q_WIDTH, True), (B_KV_WIDTH, True), (B_KV_WIDTH, False)):
        for off in range(0, width, NORM_SLAB):
            slabs.append((col + off, min(NORM_SLAB, width - off), normed))
        col += width
    return tuple(slabs)


def _ffn1_proj_kernel(x_ref, g1_ref, wgu_ref, wd_ref, gm_ref, win_ref, pool_ref, qkg_ref,
                      h1_ref, qkv_ref, act_ref):
    x = x_ref[...]
    h1 = x + 0.5 * _swiglu(_rms_norm(x, g1_ref[...]).astype(BF16), wgu_ref, wd_ref, act_ref)
    h1_ref[...] = h1
    u = _rms_norm(h1, gm_ref[...]).astype(BF16)
    for lo, w, normed in _proj_slabs():
        y = _dot(u, win_ref[:, lo:lo + w])
        if normed:
            y = _head_group_norm(y, pool_ref[:w, :w], qkg_ref[:, lo:lo + w])
        qkv_ref[:, lo:lo + w] = y.astype(BF16)


def _merge_kernel(h1_ref, ya_ref, yb_ref, gm_ref, wg_ref, wa_ref, wb_ref, wo_ref, h2_ref):
    h1 = h1_ref[...]
    u = _rms_norm(h1, gm_ref[...]).astype(BF16)
    ga = _sigmoid(_dot(u, wg_ref[:, :D_MODEL]))
    gb = _sigmoid(_dot(u, wg_ref[:, D_MODEL:]))
    merged = ga * _dot(ya_ref[...], wa_ref[...]) + gb * _dot(yb_ref[...], wb_ref[...])
    h2_ref[...] = h1 + _dot(merged.astype(BF16), wo_ref[...])


def _ffn2_ple_kernel(h2_ref, p_ref, g2_ref, wgu_ref, wd_ref, gp_ref, wpg_ref, wpe_ref,
                     out_ref, act_ref):
    h2 = h2_ref[...]
    h3 = h2 + 0.5 * _swiglu(_rms_norm(h2, g2_ref[...]).astype(BF16), wgu_ref, wd_ref, act_ref)
    gate = _sigmoid(_dot(_rms_norm(h3, gp_ref[...]).astype(BF16), wpg_ref[...]))
    out_ref[...] = h3 + gate * _dot(p_ref[...].astype(BF16), wpe_ref[...])


def _band_attention_kernel(*refs, n_heads, kv_group, pad, has_sink):
    if has_sink:
        sink_ref, q_ref, k_ref, v_ref, bias_ref, o_ref, kpad_ref, vpad_ref = refs
    else:
        q_ref, k_ref, v_ref, bias_ref, o_ref, kpad_ref, vpad_ref = refs
    t = pl.program_id(1)
    window = Q_TILE + pad

    @pl.when(t == 0)
    def _():
        kpad_ref[:pad, :] = jnp.zeros((pad, kpad_ref.shape[1]), BF16)
        vpad_ref[:pad, :] = jnp.zeros((pad, vpad_ref.shape[1]), BF16)
        kpad_ref[pad:, :] = k_ref[0]
        vpad_ref[pad:, :] = v_ref[0]

    start = pl.multiple_of(t * Q_TILE, Q_TILE)
    col = lax.broadcasted_iota(jnp.int32, (Q_TILE, window), 1)
    exists = col >= pad - start

    for h in range(n_heads):
        kv = h // kv_group
        q = q_ref[0, :, h * HEAD_DIM:(h + 1) * HEAD_DIM]
        k = kpad_ref[pl.ds(start, window), kv * HEAD_DIM:(kv + 1) * HEAD_DIM]
        v = vpad_ref[pl.ds(start, window), kv * HEAD_DIM:(kv + 1) * HEAD_DIM]
        s = lax.dot_general(q, k, (((1,), (1,)), ((), ())), preferred_element_type=F32)
        s = jnp.where(exists, s + bias_ref[h], NEG_INF)
        m = jnp.max(s, axis=-1, keepdims=True)
        if has_sink:
            m = jnp.maximum(m, sink_ref[h])
        p = jnp.exp(s - m)
        denom = jnp.sum(p, axis=-1, keepdims=True)
        if has_sink:
            denom = denom + jnp.exp(sink_ref[h] - m)
        o = _dot(p.astype(BF16), v) / denom
        o_ref[0, :, h * HEAD_DIM:(h + 1) * HEAD_DIM] = o.astype(BF16)


def _resident(shape):
    return pl.BlockSpec(shape, lambda *_: (0,) * len(shape), pipeline_mode=pl.Buffered(1))


def _row_tiled(width):
    return pl.BlockSpec((ROW_TILE, width), lambda i: (i, 0))


def _dense_call(kernel, n_rows, in_specs, out_specs, out_shape, scratch_shapes, name):
    return pl.pallas_call(
        kernel,
        grid=(n_rows // ROW_TILE,),
        in_specs=in_specs,
        out_specs=out_specs,
        out_shape=out_shape,
        scratch_shapes=scratch_shapes,
        compiler_params=pltpu.CompilerParams(
            dimension_semantics=("arbitrary",), vmem_limit_bytes=VMEM_LIMIT_BYTES),
        name=name,
    )


def _band_attention(qkv, bias, sinks, *, q_col, k_col, v_col, n_heads, n_kv, n_prev, name):
    b, s, _ = qkv.shape
    pad = n_prev * CHUNK
    qw, kw = n_heads * HEAD_DIM, n_kv * HEAD_DIM
    has_sink = sinks is not None
    in_specs = [
        pl.BlockSpec((1, Q_TILE, qw), lambda bi, ti: (bi, ti, q_col)),
        pl.BlockSpec((1, s, kw), lambda bi, ti: (bi, 0, k_col)),
        pl.BlockSpec((1, s, kw), lambda bi, ti: (bi, 0, v_col)),
        pl.BlockSpec(bias.shape, lambda bi, ti: (0, 0, 0), pipeline_mode=pl.Buffered(1)),
    ]
    args = [qkv, qkv, qkv, bias]
    if has_sink:
        in_specs.insert(0, pl.BlockSpec(memory_space=pltpu.SMEM))
        args.insert(0, sinks)
    return pl.pallas_call(
        functools.partial(_band_attention_kernel, n_heads=n_heads, kv_group=n_heads // n_kv,
                          pad=pad, has_sink=has_sink),
        grid=(b, s // Q_TILE),
        in_specs=in_specs,
        out_specs=pl.BlockSpec((1, Q_TILE, qw), lambda bi, ti: (bi, ti, 0)),
        out_shape=jax.ShapeDtypeStruct((b, s, qw), BF16),
        scratch_shapes=[pltpu.VMEM((s + pad, kw), BF16), pltpu.VMEM((s + pad, kw), BF16)],
        compiler_params=pltpu.CompilerParams(
            dimension_semantics=("arbitrary", "arbitrary"), vmem_limit_bytes=VMEM_LIMIT_BYTES),
        name=name,
    )(*args)


def _band_geometry(n_prev):
    i = np.arange(Q_TILE)[:, None]
    j = np.arange(Q_TILE + n_prev * CHUNK)[None, :]
    dist = i + n_prev * CHUNK - j
    q_chunk, k_chunk = i // CHUNK, j // CHUNK
    in_band = (k_chunk >= q_chunk) & (k_chunk <= q_chunk + n_prev)
    return dist, in_band


def _alibi_slopes(n_heads):
    return np.array([2.0 ** (-8.0 * (h + 1) / n_heads) for h in range(n_heads)], dtype=np.float32)


def kernel(x, p, ffn1_norm, ffn1_w_gu, ffn1_w_down, mix_norm, w_in, a_q_norm, a_k_norm, a_rel_bias, b_q_norm, b_k_norm, b_sinks, w_gate, w_proj_a, w_proj_b, w_out, ffn2_norm, ffn2_w_gu, ffn2_w_down, ple_norm, w_ple_gate, w_ple_proj):
    b, s, d = x.shape
    depth = ffn1_w_gu.shape[0]
    assert d == D_MODEL and s % Q_TILE == 0 and (b * s) % ROW_TILE == 0
    n_rows = b * s
    scale = HEAD_DIM ** -0.5

    dist_a, band_a = _band_geometry(A_PREV_CHUNKS)
    rel_idx = np.clip(dist_a, -A_MAX_REL, A_MAX_REL) + A_MAX_REL
    dist_b, band_b = _band_geometry(B_PREV_CHUNKS)
    alibi = -_alibi_slopes(B_Q_HEADS)[:, None, None] * np.abs(dist_b).astype(np.float32)[None]
    bias_b = jnp.asarray(np.where(band_b[None], alibi, np.float32(NEG_INF)))
    pool = jnp.asarray(np.kron(np.eye(NORM_SLAB // HEAD_DIM), np.full((HEAD_DIM, HEAD_DIM), 1.0 / HEAD_DIM)),
                       dtype=BF16)

    row_f32 = jax.ShapeDtypeStruct((n_rows, D_MODEL), F32)
    act_scratch = pltpu.VMEM((ROW_TILE, D_FF), BF16)

    h = x.reshape(n_rows, D_MODEL)
    for i in range(depth):
        vec = lambda g: g[i].astype(F32).reshape(1, -1)
        qk_gain = jnp.concatenate([
            jnp.tile(a_q_norm[i], A_HEADS) * scale, jnp.tile(a_k_norm[i], A_HEADS),
            jnp.ones((A_WIDTH,), F32),
            jnp.tile(b_q_norm[i], B_Q_HEADS) * scale, jnp.tile(b_k_norm[i], B_KV_HEADS),
            jnp.ones((B_KV_WIDTH,), F32)]).astype(F32).reshape(1, IN_COLS)

        h1, qkv = _dense_call(
            _ffn1_proj_kernel, n_rows,
            in_specs=[_row_tiled(D_MODEL), _resident((1, D_MODEL)), _resident((D_MODEL, 2 * D_FF)),
                      _resident((D_FF, D_MODEL)), _resident((1, D_MODEL)), _resident((D_MODEL, IN_COLS)),
                      _resident((NORM_SLAB, NORM_SLAB)), _resident((1, IN_COLS))],
            out_specs=[_row_tiled(D_MODEL), _row_tiled(IN_COLS)],
            out_shape=[row_f32, jax.ShapeDtypeStruct((n_rows, IN_COLS), BF16)],
            scratch_shapes=[act_scratch], name="ffn1_proj",
        )(h, vec(ffn1_norm), ffn1_w_gu[i].astype(BF16), ffn1_w_down[i].astype(BF16),
          vec(mix_norm), w_in[i].astype(BF16), pool, qk_gain)

        qkv3 = qkv.reshape(b, s, IN_COLS)
        bias_a = jnp.where(band_a[None], a_rel_bias[i].astype(F32)[:, rel_idx], NEG_INF)
        ya = _band_attention(qkv3, bias_a, None, q_col=0, k_col=1, v_col=2,
                             n_heads=A_HEADS, n_kv=A_HEADS, n_prev=A_PREV_CHUNKS, name="mixer_a")
        kb_col = (3 * A_WIDTH + B_Q_WIDTH) // B_KV_WIDTH
        yb = _band_attention(qkv3, bias_b, b_sinks[i].astype(F32), q_col=3, k_col=kb_col, v_col=kb_col + 1,
                             n_heads=B_Q_HEADS, n_kv=B_KV_HEADS, n_prev=B_PREV_CHUNKS, name="mixer_b")

        h2 = _dense_call(
            _merge_kernel, n_rows,
            in_specs=[_row_tiled(D_MODEL), _row_tiled(A_WIDTH), _row_tiled(B_Q_WIDTH), _resident((1, D_MODEL)),
                      _resident((D_MODEL, 2 * D_MODEL)), _resident((A_WIDTH, D_MODEL)),
                      _resident((B_Q_WIDTH, D_MODEL)), _resident((D_MODEL, D_MODEL))],
            out_specs=_row_tiled(D_MODEL), out_shape=row_f32, scratch_shapes=[], name="merge",
        )(h1, ya.reshape(n_rows, A_WIDTH), yb.reshape(n_rows, B_Q_WIDTH), vec(mix_norm),
          w_gate[i].astype(BF16), w_proj_a[i].astype(BF16), w_proj_b[i].astype(BF16), w_out[i].astype(BF16))

        h = _dense_call(
            _ffn2_ple_kernel, n_rows,
            in_specs=[_row_tiled(D_MODEL), _row_tiled(PLE_DIM), _resident((1, D_MODEL)),
                      _resident((D_MODEL, 2 * D_FF)), _resident((D_FF, D_MODEL)), _resident((1, D_MODEL)),
                      _resident((D_MODEL, D_MODEL)), _resident((PLE_DIM, D_MODEL))],
            out_specs=_row_tiled(D_MODEL), out_shape=row_f32, scratch_shapes=[act_scratch], name="ffn2_ple",
        )(h2, p[i].reshape(n_rows, PLE_DIM), vec(ffn2_norm), ffn2_w_gu[i].astype(BF16),
          ffn2_w_down[i].astype(BF16), vec(ple_norm), w_ple_gate[i].astype(BF16), w_ple_proj[i].astype(BF16))
    return h.reshape(b, s, D_MODEL)
```

```python
import functools

import numpy as np
import jax
import jax.numpy as jnp
from jax import lax
from jax.experimental import pallas as pl
from jax.experimental.pallas import tpu as pltpu

D_MODEL = 1024
D_FF = 2816
PLE_DIM = 256
CHUNK = 64
HEAD_DIM = 64
A_HEADS = 8
A_PREV_CHUNKS = 8
A_MAX_REL = 128
B_Q_HEADS = 8
B_KV_HEADS = 2
B_GROUP = B_Q_HEADS // B_KV_HEADS
B_PREV_CHUNKS = 2
A_WIDTH = A_HEADS * HEAD_DIM
B_Q_WIDTH = B_Q_HEADS * HEAD_DIM
B_KV_WIDTH = B_KV_HEADS * HEAD_DIM
IN_COLS = 3 * A_WIDTH + B_Q_WIDTH + 2 * B_KV_WIDTH
EPS = 1e-6
NEG_INF = -1e30
LOG2_E = 1.4426950408889634

V7X_MXU_DIM = 256
V7X_LANES = 128
V7X_VMEM_BYTES = 64 * 1024 * 1024
VMEM_LIMIT_BYTES = V7X_VMEM_BYTES - 8 * 1024 * 1024

ROW_TILE = 512
FF_CHUNK = V7X_MXU_DIM
Q_TILE = 2 * CHUNK
NORM_SLAB = V7X_MXU_DIM

BF16 = jnp.bfloat16
F32 = jnp.float32


def _dot(a, b):
    return jnp.dot(a, b, preferred_element_type=F32)


def _sigmoid(x):
    return 1.0 / (1.0 + jnp.exp(-x))


def _rms_norm(h, gain):
    ms = jnp.mean(h * h, axis=-1, keepdims=True)
    return (h * lax.rsqrt(ms + EPS)) * gain


def _swiglu(xn, wgu_ref, wd_ref, act_ref):
    for j in range(D_FF // FF_CHUNK):
        lo = j * FF_CHUNK
        g = _dot(xn, wgu_ref[:, lo:lo + FF_CHUNK])
        u = _dot(xn, wgu_ref[:, D_FF + lo:D_FF + lo + FF_CHUNK])
        act_ref[:, lo:lo + FF_CHUNK] = ((g * _sigmoid(g)) * u).astype(BF16)
    return _dot(act_ref[...], wd_ref[...])


def _head_group_norm(x, pool, gain):
    ms = _dot((x * x).astype(BF16), pool)
    return (x * lax.rsqrt(ms + EPS)) * gain


def _proj_slabs():
    slabs = []
    col = 0
    for width, normed in ((A_WIDTH, True), (A_WIDTH, True), (A_WIDTH, False),
                          (B_Q_WIDTH, True), (B_KV_WIDTH, True), (B_KV_WIDTH, False)):
        for off in range(0, width, NORM_SLAB):
            slabs.append((col + off, min(NORM_SLAB, width - off), normed))
        col += width
    return tuple(slabs)


def _ffn1_proj_kernel(x_ref, g1_ref, wgu_ref, wd_ref, gm_ref, win_ref, pool_ref, qkg_ref,
                      h1_ref, qkv_ref, act_ref):
    x = x_ref[...]
    h1 = x + 0.5 * _swiglu(_rms_norm(x, g1_ref[...]).astype(BF16), wgu_ref, wd_ref, act_ref)
    h1_ref[...] = h1
    u = _rms_norm(h1, gm_ref[...]).astype(BF16)
    for lo, w, normed in _proj_slabs():
        y = _dot(u, win_ref[:, lo:lo + w])
        if normed:
            y = _head_group_norm(y, pool_ref[:w, :w], qkg_ref[:, lo:lo + w])
        qkv_ref[:, lo:lo + w] = y.astype(BF16)


def _merge_kernel(h1_ref, ya_ref, yb_ref, gm_ref, wg_ref, wa_ref, wb_ref, wo_ref, h2_ref):
    h1 = h1_ref[...]
    u = _rms_norm(h1, gm_ref[...]).astype(BF16)
    ga = _sigmoid(_dot(u, wg_ref[:, :D_MODEL]))
    gb = _sigmoid(_dot(u, wg_ref[:, D_MODEL:]))
    merged = ga * _dot(ya_ref[...], wa_ref[...]) + gb * _dot(yb_ref[...], wb_ref[...])
    h2_ref[...] = h1 + _dot(merged.astype(BF16), wo_ref[...])


def _ffn2_ple_kernel(h2_ref, p_ref, g2_ref, wgu_ref, wd_ref, gp_ref, wpg_ref, wpe_ref,
                     out_ref, act_ref):
    h2 = h2_ref[...]
    h3 = h2 + 0.5 * _swiglu(_rms_norm(h2, g2_ref[...]).astype(BF16), wgu_ref, wd_ref, act_ref)
    gate = _sigmoid(_dot(_rms_norm(h3, gp_ref[...]).astype(BF16), wpg_ref[...]))
    out_ref[...] = h3 + gate * _dot(p_ref[...].astype(BF16), wpe_ref[...])


def _band_attention_kernel(*refs, n_pairs, kv_pairs, pad, has_sink):
    if has_sink:
        sink_ref, q_ref, k_ref, v_ref, bias_ref, o_ref, kpad_ref, vext_ref = refs
    else:
        q_ref, k_ref, v_ref, bias_ref, o_ref, kpad_ref, vext_ref = refs
    t = pl.program_id(1)
    window = Q_TILE + pad
    seq = k_ref.shape[1]
    pair = 2 * HEAD_DIM

    @pl.when(t == 0)
    def _():
        kpad_ref[:pad, :] = jnp.zeros((pad, kpad_ref.shape[1]), BF16)
        kpad_ref[pad:, :] = k_ref[0]
        for kp in range(kv_pairs):
            vext_ref[:pad, 2 * pair * kp:2 * pair * kp + pair] = jnp.zeros((pad, pair), BF16)
            vext_ref[pad:, 2 * pair * kp:2 * pair * kp + pair] = v_ref[0, :, pair * kp:pair * (kp + 1)]
            vext_ref[:, 2 * pair * kp + pair:2 * pair * (kp + 1)] = jnp.ones((seq + pad, pair), BF16)

    start = pl.multiple_of(t * Q_TILE, Q_TILE)
    first_head = lax.broadcasted_iota(jnp.int32, (Q_TILE, pair), 1) < HEAD_DIM
    first_rows = lax.broadcasted_iota(jnp.int32, (2 * Q_TILE, 1), 0) < Q_TILE

    for pp in range(n_pairs):
        kp = pp % kv_pairs
        q = q_ref[0, :, pair * pp:pair * (pp + 1)].astype(F32)
        q2 = jnp.concatenate([jnp.where(first_head, q, 0.0), jnp.where(first_head, 0.0, q)], axis=0)
        k = kpad_ref[pl.ds(start, window), pair * kp:pair * (kp + 1)]
        v = vext_ref[pl.ds(start, window), 2 * pair * kp:2 * pair * (kp + 1)]
        s = lax.dot_general(q2.astype(BF16), k, (((1,), (1,)), ((), ())), preferred_element_type=F32)
        s = s + bias_ref[0, 2 * pp:2 * pp + 2].reshape(2 * Q_TILE, window)
        m = jnp.max(s, axis=-1, keepdims=True)
        if has_sink:
            sink = jnp.where(first_rows, sink_ref[2 * pp], sink_ref[2 * pp + 1])
            m = jnp.maximum(m, sink)
        o2 = _dot(jnp.exp2(s - m).astype(BF16), v)
        denom = o2[:, pair:]
        if has_sink:
            denom = denom + jnp.exp2(sink - m)
        o2 = o2[:, :pair] / denom
        o_ref[0, :, pair * pp:pair * (pp + 1)] = jnp.where(first_head, o2[:Q_TILE], o2[Q_TILE:]).astype(BF16)


def _resident(shape):
    return pl.BlockSpec(shape, lambda *_: (0,) * len(shape), pipeline_mode=pl.Buffered(1))


def _row_tiled(width):
    return pl.BlockSpec((ROW_TILE, width), lambda i: (i, 0))


def _dense_call(kernel, n_rows, in_specs, out_specs, out_shape, scratch_shapes, name):
    return pl.pallas_call(
        kernel,
        grid=(n_rows // ROW_TILE,),
        in_specs=in_specs,
        out_specs=out_specs,
        out_shape=out_shape,
        scratch_shapes=scratch_shapes,
        compiler_params=pltpu.CompilerParams(
            dimension_semantics=("arbitrary",), vmem_limit_bytes=VMEM_LIMIT_BYTES),
        name=name,
    )


def _band_attention(qkv, bias, sinks, *, q_col, k_col, v_col, n_heads, n_kv, n_prev, name):
    b, s, _ = qkv.shape
    pad = n_prev * CHUNK
    qw, kw = n_heads * HEAD_DIM, n_kv * HEAD_DIM
    last_variant = bias.shape[0] - 1
    has_sink = sinks is not None
    in_specs = [
        pl.BlockSpec((1, Q_TILE, qw), lambda bi, ti: (bi, ti, q_col)),
        pl.BlockSpec((1, s, kw), lambda bi, ti: (bi, 0, k_col)),
        pl.BlockSpec((1, s, kw), lambda bi, ti: (bi, 0, v_col)),
        pl.BlockSpec((1,) + bias.shape[1:], lambda bi, ti: (jnp.minimum(ti, last_variant), 0, 0, 0)),
    ]
    args = [qkv, qkv, qkv, bias]
    if has_sink:
        in_specs.insert(0, pl.BlockSpec(memory_space=pltpu.SMEM))
        args.insert(0, sinks)
    return pl.pallas_call(
        functools.partial(_band_attention_kernel, n_pairs=n_heads // 2, kv_pairs=n_kv // 2,
                          pad=pad, has_sink=has_sink),
        grid=(b, s // Q_TILE),
        in_specs=in_specs,
        out_specs=pl.BlockSpec((1, Q_TILE, qw), lambda bi, ti: (bi, ti, 0)),
        out_shape=jax.ShapeDtypeStruct((b, s, qw), BF16),
        scratch_shapes=[pltpu.VMEM((s + pad, kw), BF16), pltpu.VMEM((s + pad, 2 * kw), BF16)],
        compiler_params=pltpu.CompilerParams(
            dimension_semantics=("arbitrary", "arbitrary"), vmem_limit_bytes=VMEM_LIMIT_BYTES),
        name=name,
    )(*args)


def _band_geometry(n_prev):
    pad = n_prev * CHUNK
    assert pad % Q_TILE == 0
    i = np.arange(Q_TILE)[:, None]
    j = np.arange(Q_TILE + pad)[None, :]
    dist = i + pad - j
    q_chunk, k_chunk = i // CHUNK, j // CHUNK
    in_band = (k_chunk >= q_chunk) & (k_chunk <= q_chunk + n_prev)
    first_col = pad - Q_TILE * np.arange(pad // Q_TILE + 1)
    visible = in_band[None] & (j[None] >= first_col[:, None, None])
    return dist, visible


def _toeplitz(g, n_rows, n_cols):
    length = n_rows + n_cols - 1
    skew = jnp.tile(g, (1, n_rows + 1))[:, :n_rows * (length + 1)].reshape(-1, n_rows, length + 1)
    return skew[:, :, :n_cols][:, :, ::-1]


def _alibi_slopes(n_heads):
    return np.array([2.0 ** (-8.0 * (h + 1) / n_heads) for h in range(n_heads)], dtype=np.float32)


def kernel(x, p, ffn1_norm, ffn1_w_gu, ffn1_w_down, mix_norm, w_in, a_q_norm, a_k_norm, a_rel_bias, b_q_norm, b_k_norm, b_sinks, w_gate, w_proj_a, w_proj_b, w_out, ffn2_norm, ffn2_w_gu, ffn2_w_down, ple_norm, w_ple_gate, w_ple_proj):
    b, s, d = x.shape
    depth = ffn1_w_gu.shape[0]
    assert d == D_MODEL and s % Q_TILE == 0 and (b * s) % ROW_TILE == 0
    n_rows = b * s
    q_scale = np.float32(HEAD_DIM ** -0.5 * LOG2_E)

    win_a = Q_TILE + A_PREV_CHUNKS * CHUNK
    _, visible_a = _band_geometry(A_PREV_CHUNKS)
    diag = np.arange(Q_TILE + win_a - 1) - (win_a - 1) + A_PREV_CHUNKS * CHUNK
    rel_idx = np.clip(diag, -A_MAX_REL, A_MAX_REL) + A_MAX_REL
    dist_b, visible_b = _band_geometry(B_PREV_CHUNKS)
    b_lane_heads = np.arange(B_Q_HEADS).reshape(B_KV_HEADS, B_GROUP).T.reshape(-1)
    alibi = -_alibi_slopes(B_Q_HEADS)[:, None, None] * np.abs(dist_b).astype(np.float32)[None]
    bias_b = jnp.asarray(np.where(visible_b[:, None], (alibi * np.float32(LOG2_E))[b_lane_heads][None],
                                  np.float32(NEG_INF)))
    pool = jnp.asarray(np.kron(np.eye(NORM_SLAB // HEAD_DIM), np.full((HEAD_DIM, HEAD_DIM), 1.0 / HEAD_DIM)),
                       dtype=BF16)

    row_f32 = jax.ShapeDtypeStruct((n_rows, D_MODEL), F32)
    act_scratch = pltpu.VMEM((ROW_TILE, D_FF), BF16)

    h = x.reshape(n_rows, D_MODEL)
    for i in range(depth):
        vec = lambda g: g[i].astype(F32).reshape(1, -1)
        qk_gain = jnp.concatenate([
            jnp.tile(a_q_norm[i], A_HEADS) * q_scale, jnp.tile(a_k_norm[i], A_HEADS),
            jnp.ones((A_WIDTH,), F32),
            jnp.tile(b_q_norm[i], B_Q_HEADS) * q_scale, jnp.tile(b_k_norm[i], B_KV_HEADS),
            jnp.ones((B_KV_WIDTH,), F32)]).astype(F32).reshape(1, IN_COLS)
        qb_lo = 3 * A_WIDTH
        w_qb = w_in[i][:, qb_lo:qb_lo + B_Q_WIDTH].reshape(D_MODEL, B_KV_HEADS, B_GROUP, HEAD_DIM)
        w_in_i = jnp.concatenate([w_in[i][:, :qb_lo], w_qb.transpose(0, 2, 1, 3).reshape(D_MODEL, B_Q_WIDTH),
                                  w_in[i][:, qb_lo + B_Q_WIDTH:]], axis=1)
        w_proj_b_i = (w_proj_b[i].reshape(B_KV_HEADS, B_GROUP, HEAD_DIM, D_MODEL)
                      .transpose(1, 0, 2, 3).reshape(B_Q_WIDTH, D_MODEL))

        h1, qkv = _dense_call(
            _ffn1_proj_kernel, n_rows,
            in_specs=[_row_tiled(D_MODEL), _resident((1, D_MODEL)), _resident((D_MODEL, 2 * D_FF)),
                      _resident((D_FF, D_MODEL)), _resident((1, D_MODEL)), _resident((D_MODEL, IN_COLS)),
                      _resident((NORM_SLAB, NORM_SLAB)), _resident((1, IN_COLS))],
            out_specs=[_row_tiled(D_MODEL), _row_tiled(IN_COLS)],
            out_shape=[row_f32, jax.ShapeDtypeStruct((n_rows, IN_COLS), BF16)],
            scratch_shapes=[act_scratch], name="ffn1_proj",
        )(h, vec(ffn1_norm), ffn1_w_gu[i].astype(BF16), ffn1_w_down[i].astype(BF16),
          vec(mix_norm), w_in_i.astype(BF16), pool, qk_gain)

        qkv3 = qkv.reshape(b, s, IN_COLS)
        rel = _toeplitz((a_rel_bias[i].astype(F32) * np.float32(LOG2_E))[:, rel_idx], Q_TILE, win_a)
        bias_a = jnp.where(visible_a[:, None], rel[None], NEG_INF)
        ya = _band_attention(qkv3, bias_a, None, q_col=0, k_col=1, v_col=2,
                             n_heads=A_HEADS, n_kv=A_HEADS, n_prev=A_PREV_CHUNKS, name="mixer_a")
        kb_col = (3 * A_WIDTH + B_Q_WIDTH) // B_KV_WIDTH
        sinks = (b_sinks[i].astype(F32) * np.float32(LOG2_E))[b_lane_heads]
        yb = _band_attention(qkv3, bias_b, sinks, q_col=3, k_col=kb_col, v_col=kb_col + 1,
                             n_heads=B_Q_HEADS, n_kv=B_KV_HEADS, n_prev=B_PREV_CHUNKS, name="mixer_b")

        h2 = _dense_call(
            _merge_kernel, n_rows,
            in_specs=[_row_tiled(D_MODEL), _row_tiled(A_WIDTH), _row_tiled(B_Q_WIDTH), _resident((1, D_MODEL)),
                      _resident((D_MODEL, 2 * D_MODEL)), _resident((A_WIDTH, D_MODEL)),
                      _resident((B_Q_WIDTH, D_MODEL)), _resident((D_MODEL, D_MODEL))],
            out_specs=_row_tiled(D_MODEL), out_shape=row_f32, scratch_shapes=[], name="merge",
        )(h1, ya.reshape(n_rows, A_WIDTH), yb.reshape(n_rows, B_Q_WIDTH), vec(mix_norm),
          w_gate[i].astype(BF16), w_proj_a[i].astype(BF16), w_proj_b_i.astype(BF16), w_out[i].astype(BF16))

        h = _dense_call(
            _ffn2_ple_kernel, n_rows,
            in_specs=[_row_tiled(D_MODEL), _row_tiled(PLE_DIM), _resident((1, D_MODEL)),
                      _resident((D_MODEL, 2 * D_FF)), _resident((D_FF, D_MODEL)), _resident((1, D_MODEL)),
                      _resident((D_MODEL, D_MODEL)), _resident((PLE_DIM, D_MODEL))],
            out_specs=_row_tiled(D_MODEL), out_shape=row_f32, scratch_shapes=[act_scratch], name="ffn2_ple",
        )(h2, p[i].reshape(n_rows, PLE_DIM), vec(ffn2_norm), ffn2_w_gu[i].astype(BF16),
          ffn2_w_down[i].astype(BF16), vec(ple_norm), w_ple_gate[i].astype(BF16), w_ple_proj[i].astype(BF16))
    return h.reshape(b, s, D_MODEL)
```

```python
import functools

import numpy as np
import jax
import jax.numpy as jnp
from jax import lax
from jax.experimental import pallas as pl
from jax.experimental.pallas import tpu as pltpu

D_MODEL = 1024
D_FF = 2816
PLE_DIM = 256
CHUNK = 64
HEAD_DIM = 64
A_HEADS = 8
A_PREV_CHUNKS = 8
A_MAX_REL = 128
B_Q_HEADS = 8
B_KV_HEADS = 2
B_GROUP = B_Q_HEADS // B_KV_HEADS
B_PREV_CHUNKS = 2
A_WIDTH = A_HEADS * HEAD_DIM
B_Q_WIDTH = B_Q_HEADS * HEAD_DIM
B_KV_WIDTH = B_KV_HEADS * HEAD_DIM
IN_COLS = 3 * A_WIDTH + B_Q_WIDTH + 2 * B_KV_WIDTH
EPS = 1e-6
NEG_INF = -1e30
LOG2_E = 1.4426950408889634

V7X_MXU_DIM = 256
V7X_LANES = 128
V7X_VMEM_BYTES = 64 * 1024 * 1024
VMEM_LIMIT_BYTES = V7X_VMEM_BYTES - 8 * 1024 * 1024

ROW_TILE = 512
FF_CHUNK = V7X_MXU_DIM
Q_TILE = 2 * CHUNK
Q_STEP = 4 * Q_TILE
NORM_SLAB = V7X_MXU_DIM

BF16 = jnp.bfloat16
F32 = jnp.float32


def _dot(a, b):
    return jnp.dot(a, b, preferred_element_type=F32)


def _sigmoid(x):
    return 1.0 / (1.0 + jnp.exp(-x))


def _rms_norm(h, gain):
    ms = jnp.mean(h * h, axis=-1, keepdims=True)
    return (h * lax.rsqrt(ms + EPS)) * gain


def _swiglu(xn, wgu_ref, wd_ref, act_ref):
    for j in range(D_FF // FF_CHUNK):
        lo = j * FF_CHUNK
        g = _dot(xn, wgu_ref[:, lo:lo + FF_CHUNK])
        u = _dot(xn, wgu_ref[:, D_FF + lo:D_FF + lo + FF_CHUNK])
        act_ref[:, lo:lo + FF_CHUNK] = ((g * _sigmoid(g)) * u).astype(BF16)
    return _dot(act_ref[...], wd_ref[...])


def _head_group_norm(x, pool, gain):
    ms = _dot((x * x).astype(BF16), pool)
    return (x * lax.rsqrt(ms + EPS)) * gain


def _proj_slabs():
    slabs = []
    col = 0
    for width, normed in ((A_WIDTH, True), (A_WIDTH, True), (A_WIDTH, False),
                          (B_Q_WIDTH, True), (B_KV_WIDTH, True), (B_KV_WIDTH, False)):
        for off in range(0, width, NORM_SLAB):
            slabs.append((col + off, min(NORM_SLAB, width - off), normed))
        col += width
    return tuple(slabs)


def _ffn1_proj_kernel(x_ref, g1_ref, wgu_ref, wd_ref, gm_ref, win_ref, pool_ref, qkg_ref,
                      h1_ref, qkv_ref, act_ref, y_ref):
    x = x_ref[...]
    h1 = x + 0.5 * _swiglu(_rms_norm(x, g1_ref[...]).astype(BF16), wgu_ref, wd_ref, act_ref)
    h1_ref[...] = h1
    y_ref[...] = _dot(_rms_norm(h1, gm_ref[...]).astype(BF16), win_ref[...])
    for lo, w, normed in _proj_slabs():
        y = y_ref[:, lo:lo + w]
        if normed:
            y = _head_group_norm(y, pool_ref[:w, :w], qkg_ref[:, lo:lo + w])
        qkv_ref[:, lo:lo + w] = y.astype(BF16)


def _merge_kernel(h1_ref, ya_ref, yb_ref, gm_ref, wg_ref, wa_ref, wb_ref, wo_ref, h2_ref):
    h1 = h1_ref[...]
    u = _rms_norm(h1, gm_ref[...]).astype(BF16)
    ga = _sigmoid(_dot(u, wg_ref[:, :D_MODEL]))
    gb = _sigmoid(_dot(u, wg_ref[:, D_MODEL:]))
    merged = ga * _dot(ya_ref[...], wa_ref[...]) + gb * _dot(yb_ref[...], wb_ref[...])
    h2_ref[...] = h1 + _dot(merged.astype(BF16), wo_ref[...])


def _ffn2_ple_kernel(h2_ref, p_ref, g2_ref, wgu_ref, wd_ref, gp_ref, wpg_ref, wpe_ref,
                     out_ref, act_ref):
    h2 = h2_ref[...]
    h3 = h2 + 0.5 * _swiglu(_rms_norm(h2, g2_ref[...]).astype(BF16), wgu_ref, wd_ref, act_ref)
    gate = _sigmoid(_dot(_rms_norm(h3, gp_ref[...]).astype(BF16), wpg_ref[...]))
    out_ref[...] = h3 + gate * _dot(p_ref[...].astype(BF16), wpe_ref[...])


def _band_attention_kernel(*refs, n_pairs, kv_pairs, pad, has_sink):
    if has_sink:
        sink_ref, q_ref, k_ref, v_ref, bias_ref, o_ref, kpad_ref, vext_ref = refs
    else:
        q_ref, k_ref, v_ref, bias_ref, o_ref, kpad_ref, vext_ref = refs
    t = pl.program_id(1)
    window = Q_TILE + pad
    seq = k_ref.shape[1]
    pair = 2 * HEAD_DIM

    @pl.when(t == 0)
    def _():
        kpad_ref[:pad, :] = jnp.zeros((pad, kpad_ref.shape[1]), BF16)
        kpad_ref[pad:, :] = k_ref[0]
        for kp in range(kv_pairs):
            vext_ref[:pad, 2 * pair * kp:2 * pair * kp + pair] = jnp.zeros((pad, pair), BF16)
            vext_ref[pad:, 2 * pair * kp:2 * pair * kp + pair] = v_ref[0, :, pair * kp:pair * (kp + 1)]
            vext_ref[:, 2 * pair * kp + pair:2 * pair * (kp + 1)] = jnp.ones((seq + pad, pair), BF16)

    first_head = lax.broadcasted_iota(jnp.int32, (Q_TILE, pair), 1) < HEAD_DIM
    first_rows = lax.broadcasted_iota(jnp.int32, (2 * Q_TILE, 1), 0) < Q_TILE

    for sub in range(Q_STEP // Q_TILE):
        tile = t * (Q_STEP // Q_TILE) + sub
        start = pl.multiple_of(tile * Q_TILE, Q_TILE)
        variant = jnp.minimum(tile, bias_ref.shape[0] - 1)
        rows = slice(sub * Q_TILE, (sub + 1) * Q_TILE)
        for pp in range(n_pairs):
            kp = pp % kv_pairs
            q = q_ref[0, rows, pair * pp:pair * (pp + 1)].astype(F32)
            q2 = jnp.concatenate([jnp.where(first_head, q, 0.0), jnp.where(first_head, 0.0, q)], axis=0)
            k = kpad_ref[pl.ds(start, window), pair * kp:pair * (kp + 1)]
            v = vext_ref[pl.ds(start, window), 2 * pair * kp:2 * pair * (kp + 1)]
            s = lax.dot_general(q2.astype(BF16), k, (((1,), (1,)), ((), ())), preferred_element_type=F32)
            s = s + bias_ref[variant, 2 * pp:2 * pp + 2].reshape(2 * Q_TILE, window)
            m = jnp.max(s, axis=-1, keepdims=True)
            if has_sink:
                sink = jnp.where(first_rows, sink_ref[2 * pp], sink_ref[2 * pp + 1])
                m = jnp.maximum(m, sink)
            o2 = _dot(jnp.exp2(s - m).astype(BF16), v)
            denom = o2[:, pair:]
            if has_sink:
                denom = denom + jnp.exp2(sink - m)
            o2 = o2[:, :pair] / denom
            o_ref[0, rows, pair * pp:pair * (pp + 1)] = (
                jnp.where(first_head, o2[:Q_TILE], o2[Q_TILE:]).astype(BF16))


def _resident(shape):
    return pl.BlockSpec(shape, lambda *_: (0,) * len(shape), pipeline_mode=pl.Buffered(1))


def _row_tiled(width):
    return pl.BlockSpec((ROW_TILE, width), lambda i: (i, 0))


def _dense_call(kernel, n_rows, in_specs, out_specs, out_shape, scratch_shapes, name):
    return pl.pallas_call(
        kernel,
        grid=(n_rows // ROW_TILE,),
        in_specs=in_specs,
        out_specs=out_specs,
        out_shape=out_shape,
        scratch_shapes=scratch_shapes,
        compiler_params=pltpu.CompilerParams(
            dimension_semantics=("arbitrary",), vmem_limit_bytes=VMEM_LIMIT_BYTES),
        name=name,
    )


def _band_attention(qkv, bias, sinks, *, q_col, k_col, v_col, n_heads, n_kv, n_prev, name):
    b, s, _ = qkv.shape
    pad = n_prev * CHUNK
    qw, kw = n_heads * HEAD_DIM, n_kv * HEAD_DIM
    has_sink = sinks is not None
    in_specs = [
        pl.BlockSpec((1, Q_STEP, qw), lambda bi, ti: (bi, ti, q_col)),
        pl.BlockSpec((1, s, kw), lambda bi, ti: (bi, 0, k_col)),
        pl.BlockSpec((1, s, kw), lambda bi, ti: (bi, 0, v_col)),
        _resident(bias.shape),
    ]
    args = [qkv, qkv, qkv, bias]
    if has_sink:
        in_specs.insert(0, pl.BlockSpec(memory_space=pltpu.SMEM))
        args.insert(0, sinks)
    return pl.pallas_call(
        functools.partial(_band_attention_kernel, n_pairs=n_heads // 2, kv_pairs=n_kv // 2,
                          pad=pad, has_sink=has_sink),
        grid=(b, s // Q_STEP),
        in_specs=in_specs,
        out_specs=pl.BlockSpec((1, Q_STEP, qw), lambda bi, ti: (bi, ti, 0)),
        out_shape=jax.ShapeDtypeStruct((b, s, qw), BF16),
        scratch_shapes=[pltpu.VMEM((s + pad, kw), BF16), pltpu.VMEM((s + pad, 2 * kw), BF16)],
        compiler_params=pltpu.CompilerParams(
            dimension_semantics=("arbitrary", "arbitrary"), vmem_limit_bytes=VMEM_LIMIT_BYTES),
        name=name,
    )(*args)


def _band_geometry(n_prev):
    pad = n_prev * CHUNK
    assert pad % Q_TILE == 0
    i = np.arange(Q_TILE)[:, None]
    j = np.arange(Q_TILE + pad)[None, :]
    dist = i + pad - j
    q_chunk, k_chunk = i // CHUNK, j // CHUNK
    in_band = (k_chunk >= q_chunk) & (k_chunk <= q_chunk + n_prev)
    first_col = pad - Q_TILE * np.arange(pad // Q_TILE + 1)
    visible = in_band[None] & (j[None] >= first_col[:, None, None])
    return dist, visible


def _toeplitz(g, n_rows, n_cols):
    length = n_rows + n_cols - 1
    skew = jnp.tile(g, (1, n_rows + 1))[:, :n_rows * (length + 1)].reshape(-1, n_rows, length + 1)
    return skew[:, :, :n_cols][:, :, ::-1]


def _alibi_slopes(n_heads):
    return np.array([2.0 ** (-8.0 * (h + 1) / n_heads) for h in range(n_heads)], dtype=np.float32)


def kernel(x, p, ffn1_norm, ffn1_w_gu, ffn1_w_down, mix_norm, w_in, a_q_norm, a_k_norm, a_rel_bias, b_q_norm, b_k_norm, b_sinks, w_gate, w_proj_a, w_proj_b, w_out, ffn2_norm, ffn2_w_gu, ffn2_w_down, ple_norm, w_ple_gate, w_ple_proj):
    b, s, d = x.shape
    depth = ffn1_w_gu.shape[0]
    assert d == D_MODEL and s % Q_STEP == 0 and (b * s) % ROW_TILE == 0
    n_rows = b * s
    q_scale = np.float32(HEAD_DIM ** -0.5 * LOG2_E)

    win_a = Q_TILE + A_PREV_CHUNKS * CHUNK
    _, visible_a = _band_geometry(A_PREV_CHUNKS)
    diag = np.arange(Q_TILE + win_a - 1) - (win_a - 1) + A_PREV_CHUNKS * CHUNK
    rel_idx = np.clip(diag, -A_MAX_REL, A_MAX_REL) + A_MAX_REL
    dist_b, visible_b = _band_geometry(B_PREV_CHUNKS)
    b_lane_heads = np.arange(B_Q_HEADS).reshape(B_KV_HEADS, B_GROUP).T.reshape(-1)
    alibi = -_alibi_slopes(B_Q_HEADS)[:, None, None] * np.abs(dist_b).astype(np.float32)[None]
    bias_b = jnp.asarray(np.where(visible_b[:, None], (alibi * np.float32(LOG2_E))[b_lane_heads][None],
                                  np.float32(NEG_INF)))
    pool = jnp.asarray(np.kron(np.eye(NORM_SLAB // HEAD_DIM), np.full((HEAD_DIM, HEAD_DIM), 1.0 / HEAD_DIM)),
                       dtype=BF16)

    row_f32 = jax.ShapeDtypeStruct((n_rows, D_MODEL), F32)
    act_scratch = pltpu.VMEM((ROW_TILE, D_FF), BF16)

    h = x.reshape(n_rows, D_MODEL)
    for i in range(depth):
        vec = lambda g: g[i].astype(F32).reshape(1, -1)
        qk_gain = jnp.concatenate([
            jnp.tile(a_q_norm[i], A_HEADS) * q_scale, jnp.tile(a_k_norm[i], A_HEADS),
            jnp.ones((A_WIDTH,), F32),
            jnp.tile(b_q_norm[i], B_Q_HEADS) * q_scale, jnp.tile(b_k_norm[i], B_KV_HEADS),
            jnp.ones((B_KV_WIDTH,), F32)]).astype(F32).reshape(1, IN_COLS)
        qb_lo = 3 * A_WIDTH
        w_qb = w_in[i][:, qb_lo:qb_lo + B_Q_WIDTH].reshape(D_MODEL, B_KV_HEADS, B_GROUP, HEAD_DIM)
        w_in_i = jnp.concatenate([w_in[i][:, :qb_lo], w_qb.transpose(0, 2, 1, 3).reshape(D_MODEL, B_Q_WIDTH),
                                  w_in[i][:, qb_lo + B_Q_WIDTH:]], axis=1)
        w_proj_b_i = (w_proj_b[i].reshape(B_KV_HEADS, B_GROUP, HEAD_DIM, D_MODEL)
                      .transpose(1, 0, 2, 3).reshape(B_Q_WIDTH, D_MODEL))

        h1, qkv = _dense_call(
            _ffn1_proj_kernel, n_rows,
            in_specs=[_row_tiled(D_MODEL), _resident((1, D_MODEL)), _resident((D_MODEL, 2 * D_FF)),
                      _resident((D_FF, D_MODEL)), _resident((1, D_MODEL)), _resident((D_MODEL, IN_COLS)),
                      _resident((NORM_SLAB, NORM_SLAB)), _resident((1, IN_COLS))],
            out_specs=[_row_tiled(D_MODEL), _row_tiled(IN_COLS)],
            out_shape=[row_f32, jax.ShapeDtypeStruct((n_rows, IN_COLS), BF16)],
            scratch_shapes=[act_scratch, pltpu.VMEM((ROW_TILE, IN_COLS), F32)], name="ffn1_proj",
        )(h, vec(ffn1_norm), ffn1_w_gu[i].astype(BF16), ffn1_w_down[i].astype(BF16),
          vec(mix_norm), w_in_i.astype(BF16), pool, qk_gain)

        qkv3 = qkv.reshape(b, s, IN_COLS)
        rel = _toeplitz((a_rel_bias[i].astype(F32) * np.float32(LOG2_E))[:, rel_idx], Q_TILE, win_a)
        bias_a = jnp.where(visible_a[:, None], rel[None], NEG_INF)
        ya = _band_attention(qkv3, bias_a, None, q_col=0, k_col=1, v_col=2,
                             n_heads=A_HEADS, n_kv=A_HEADS, n_prev=A_PREV_CHUNKS, name="mixer_a")
        kb_col = (3 * A_WIDTH + B_Q_WIDTH) // B_KV_WIDTH
        sinks = (b_sinks[i].astype(F32) * np.float32(LOG2_E))[b_lane_heads]
        yb = _band_attention(qkv3, bias_b, sinks, q_col=3, k_col=kb_col, v_col=kb_col + 1,
                             n_heads=B_Q_HEADS, n_kv=B_KV_HEADS, n_prev=B_PREV_CHUNKS, name="mixer_b")

        h2 = _dense_call(
            _merge_kernel, n_rows,
            in_specs=[_row_tiled(D_MODEL), _row_tiled(A_WIDTH), _row_tiled(B_Q_WIDTH), _resident((1, D_MODEL)),
                      _resident((D_MODEL, 2 * D_MODEL)), _resident((A_WIDTH, D_MODEL)),
                      _resident((B_Q_WIDTH, D_MODEL)), _resident((D_MODEL, D_MODEL))],
            out_specs=_row_tiled(D_MODEL), out_shape=row_f32, scratch_shapes=[], name="merge",
        )(h1, ya.reshape(n_rows, A_WIDTH), yb.reshape(n_rows, B_Q_WIDTH), vec(mix_norm),
          w_gate[i].astype(BF16), w_proj_a[i].astype(BF16), w_proj_b_i.astype(BF16), w_out[i].astype(BF16))

        h = _dense_call(
            _ffn2_ple_kernel, n_rows,
            in_specs=[_row_tiled(D_MODEL), _row_tiled(PLE_DIM), _resident((1, D_MODEL)),
                      _resident((D_MODEL, 2 * D_FF)), _resident((D_FF, D_MODEL)), _resident((1, D_MODEL)),
                      _resident((D_MODEL, D_MODEL)), _resident((PLE_DIM, D_MODEL))],
            out_specs=_row_tiled(D_MODEL), out_shape=row_f32, scratch_shapes=[act_scratch], name="ffn2_ple",
        )(h2, p[i].reshape(n_rows, PLE_DIM), vec(ffn2_norm), ffn2_w_gu[i].astype(BF16),
          ffn2_w_down[i].astype(BF16), vec(ple_norm), w_ple_gate[i].astype(BF16), w_ple_proj[i].astype(BF16))
    return h.reshape(b, s, D_MODEL)
```

```python
import functools

import numpy as np
import jax
import jax.numpy as jnp
from jax import lax
from jax.experimental import pallas as pl
from jax.experimental.pallas import tpu as pltpu

D_MODEL = 1024
D_FF = 2816
PLE_DIM = 256
CHUNK = 64
HEAD_DIM = 64
A_HEADS = 8
A_PREV_CHUNKS = 8
A_MAX_REL = 128
B_Q_HEADS = 8
B_KV_HEADS = 2
B_GROUP = B_Q_HEADS // B_KV_HEADS
B_PREV_CHUNKS = 2
A_WIDTH = A_HEADS * HEAD_DIM
B_Q_WIDTH = B_Q_HEADS * HEAD_DIM
B_KV_WIDTH = B_KV_HEADS * HEAD_DIM
IN_COLS = 3 * A_WIDTH + B_Q_WIDTH + 2 * B_KV_WIDTH
EPS = 1e-6
NEG_INF = -1e30
LOG2_E = 1.4426950408889634

V7X_MXU_DIM = 256
V7X_LANES = 128
V7X_VMEM_BYTES = 64 * 1024 * 1024
VMEM_LIMIT_BYTES = V7X_VMEM_BYTES - 8 * 1024 * 1024

ROW_TILE = 512
FF_CHUNK = V7X_MXU_DIM
Q_TILE = 2 * CHUNK
Q_STEP = 4 * Q_TILE
NORM_SLAB = V7X_MXU_DIM
SCORE_LOOKAHEAD = 1

BF16 = jnp.bfloat16
F32 = jnp.float32


def _dot(a, b):
    return jnp.dot(a, b, preferred_element_type=F32)


def _sigmoid(x):
    return 1.0 / (1.0 + jnp.exp(-x))


def _rms_norm(h, gain):
    ms = jnp.mean(h * h, axis=-1, keepdims=True)
    return (h * lax.rsqrt(ms + EPS)) * gain


def _swiglu(xn, wgu_ref, wd_ref, act_ref):
    for j in range(D_FF // FF_CHUNK):
        lo = j * FF_CHUNK
        g = _dot(xn, wgu_ref[:, lo:lo + FF_CHUNK])
        u = _dot(xn, wgu_ref[:, D_FF + lo:D_FF + lo + FF_CHUNK])
        act_ref[:, lo:lo + FF_CHUNK] = ((g * _sigmoid(g)) * u).astype(BF16)
    return _dot(act_ref[...], wd_ref[...])


def _head_group_norm(x, pool, gain):
    ms = _dot((x * x).astype(BF16), pool)
    return (x * lax.rsqrt(ms + EPS)) * gain


def _proj_slabs():
    slabs = []
    col = 0
    for width, normed in ((A_WIDTH, True), (A_WIDTH, True), (A_WIDTH, False),
                          (B_Q_WIDTH, True), (B_KV_WIDTH, True), (B_KV_WIDTH, False)):
        for off in range(0, width, NORM_SLAB):
            slabs.append((col + off, min(NORM_SLAB, width - off), normed))
        col += width
    return tuple(slabs)


def _ffn1_proj_kernel(x_ref, g1_ref, wgu_ref, wd_ref, gm_ref, win_ref, pool_ref, qkg_ref,
                      h1_ref, qkv_ref, act_ref, y_ref):
    x = x_ref[...]
    h1 = x + 0.5 * _swiglu(_rms_norm(x, g1_ref[...]).astype(BF16), wgu_ref, wd_ref, act_ref)
    h1_ref[...] = h1
    y_ref[...] = _dot(_rms_norm(h1, gm_ref[...]).astype(BF16), win_ref[...])
    for lo, w, normed in _proj_slabs():
        y = y_ref[:, lo:lo + w]
        if normed:
            y = _head_group_norm(y, pool_ref[:w, :w], qkg_ref[:, lo:lo + w])
        qkv_ref[:, lo:lo + w] = y.astype(BF16)


def _merge_kernel(h1_ref, ya_ref, yb_ref, gm_ref, wg_ref, wa_ref, wb_ref, wo_ref, h2_ref):
    h1 = h1_ref[...]
    u = _rms_norm(h1, gm_ref[...]).astype(BF16)
    ga = _sigmoid(_dot(u, wg_ref[:, :D_MODEL]))
    gb = _sigmoid(_dot(u, wg_ref[:, D_MODEL:]))
    merged = ga * _dot(ya_ref[...], wa_ref[...]) + gb * _dot(yb_ref[...], wb_ref[...])
    h2_ref[...] = h1 + _dot(merged.astype(BF16), wo_ref[...])


def _ffn2_ple_kernel(h2_ref, p_ref, g2_ref, wgu_ref, wd_ref, gp_ref, wpg_ref, wpe_ref,
                     out_ref, act_ref):
    h2 = h2_ref[...]
    h3 = h2 + 0.5 * _swiglu(_rms_norm(h2, g2_ref[...]).astype(BF16), wgu_ref, wd_ref, act_ref)
    gate = _sigmoid(_dot(_rms_norm(h3, gp_ref[...]).astype(BF16), wpg_ref[...]))
    out_ref[...] = h3 + gate * _dot(p_ref[...].astype(BF16), wpe_ref[...])


def _band_attention_kernel(*refs, n_pairs, kv_pairs, pad, has_sink):
    if has_sink:
        sink_ref, q_ref, k_ref, v_ref, bias_ref, o_ref, kpad_ref, vext_ref = refs
    else:
        q_ref, k_ref, v_ref, bias_ref, o_ref, kpad_ref, vext_ref = refs
    t = pl.program_id(1)
    window = Q_TILE + pad
    seq = k_ref.shape[1]
    pair = 2 * HEAD_DIM

    @pl.when(t == 0)
    def _():
        kpad_ref[:pad, :] = jnp.zeros((pad, kpad_ref.shape[1]), BF16)
        kpad_ref[pad:, :] = k_ref[0]
        for kp in range(kv_pairs):
            vext_ref[:pad, 2 * pair * kp:2 * pair * kp + pair] = jnp.zeros((pad, pair), BF16)
            vext_ref[pad:, 2 * pair * kp:2 * pair * kp + pair] = v_ref[0, :, pair * kp:pair * (kp + 1)]
            vext_ref[:, 2 * pair * kp + pair:2 * pair * (kp + 1)] = jnp.ones((seq + pad, pair), BF16)

    first_head = lax.broadcasted_iota(jnp.int32, (Q_TILE, pair), 1) < HEAD_DIM
    first_rows = lax.broadcasted_iota(jnp.int32, (2 * Q_TILE, 1), 0) < Q_TILE
    tiles_per_step = Q_STEP // Q_TILE

    def window_start(sub):
        return pl.multiple_of((t * tiles_per_step + sub) * Q_TILE, Q_TILE)

    def scores(sub, pp):
        kp = pp % kv_pairs
        variant = jnp.minimum(t * tiles_per_step + sub, bias_ref.shape[0] - 1)
        q = q_ref[0, sub * Q_TILE:(sub + 1) * Q_TILE, pair * pp:pair * (pp + 1)].astype(F32)
        q2 = jnp.concatenate([jnp.where(first_head, q, 0.0), jnp.where(first_head, 0.0, q)], axis=0)
        k = kpad_ref[pl.ds(window_start(sub), window), pair * kp:pair * (kp + 1)]
        s = lax.dot_general(q2.astype(BF16), k, (((1,), (1,)), ((), ())), preferred_element_type=F32)
        return s + bias_ref[variant, 2 * pp:2 * pp + 2].reshape(2 * Q_TILE, window)

    def finish(sub, pp, s):
        kp = pp % kv_pairs
        v = vext_ref[pl.ds(window_start(sub), window), 2 * pair * kp:2 * pair * (kp + 1)]
        m = jnp.max(s, axis=-1, keepdims=True)
        if has_sink:
            sink = jnp.where(first_rows, sink_ref[2 * pp], sink_ref[2 * pp + 1])
            m = jnp.maximum(m, sink)
        o2 = _dot(jnp.exp2(s - m).astype(BF16), v)
        denom = o2[:, pair:]
        if has_sink:
            denom = denom + jnp.exp2(sink - m)
        o2 = o2[:, :pair] / denom
        o_ref[0, sub * Q_TILE:(sub + 1) * Q_TILE, pair * pp:pair * (pp + 1)] = (
            jnp.where(first_head, o2[:Q_TILE], o2[Q_TILE:]).astype(BF16))

    units = [(sub, pp) for sub in range(tiles_per_step) for pp in range(n_pairs)]
    pending = [scores(*u) for u in units[:SCORE_LOOKAHEAD]]
    for n, u in enumerate(units):
        if n + SCORE_LOOKAHEAD < len(units):
            pending.append(scores(*units[n + SCORE_LOOKAHEAD]))
        finish(*u, pending.pop(0))


def _resident(shape):
    return pl.BlockSpec(shape, lambda *_: (0,) * len(shape), pipeline_mode=pl.Buffered(1))


def _row_tiled(width):
    return pl.BlockSpec((ROW_TILE, width), lambda i: (i, 0))


def _dense_call(kernel, n_rows, in_specs, out_specs, out_shape, scratch_shapes, name):
    return pl.pallas_call(
        kernel,
        grid=(n_rows // ROW_TILE,),
        in_specs=in_specs,
        out_specs=out_specs,
        out_shape=out_shape,
        scratch_shapes=scratch_shapes,
        compiler_params=pltpu.CompilerParams(
            dimension_semantics=("arbitrary",), vmem_limit_bytes=VMEM_LIMIT_BYTES),
        name=name,
    )


def _band_attention(qkv, bias, sinks, *, q_col, k_col, v_col, n_heads, n_kv, n_prev, name):
    b, s, _ = qkv.shape
    pad = n_prev * CHUNK
    qw, kw = n_heads * HEAD_DIM, n_kv * HEAD_DIM
    has_sink = sinks is not None
    in_specs = [
        pl.BlockSpec((1, Q_STEP, qw), lambda bi, ti: (bi, ti, q_col)),
        pl.BlockSpec((1, s, kw), lambda bi, ti: (bi, 0, k_col)),
        pl.BlockSpec((1, s, kw), lambda bi, ti: (bi, 0, v_col)),
        _resident(bias.shape),
    ]
    args = [qkv, qkv, qkv, bias]
    if has_sink:
        in_specs.insert(0, pl.BlockSpec(memory_space=pltpu.SMEM))
        args.insert(0, sinks)
    return pl.pallas_call(
        functools.partial(_band_attention_kernel, n_pairs=n_heads // 2, kv_pairs=n_kv // 2,
                          pad=pad, has_sink=has_sink),
        grid=(b, s // Q_STEP),
        in_specs=in_specs,
        out_specs=pl.BlockSpec((1, Q_STEP, qw), lambda bi, ti: (bi, ti, 0)),
        out_shape=jax.ShapeDtypeStruct((b, s, qw), BF16),
        scratch_shapes=[pltpu.VMEM((s + pad, kw), BF16), pltpu.VMEM((s + pad, 2 * kw), BF16)],
        compiler_params=pltpu.CompilerParams(
            dimension_semantics=("arbitrary", "arbitrary"), vmem_limit_bytes=VMEM_LIMIT_BYTES),
        name=name,
    )(*args)


def _band_geometry(n_prev):
    pad = n_prev * CHUNK
    assert pad % Q_TILE == 0
    i = np.arange(Q_TILE)[:, None]
    j = np.arange(Q_TILE + pad)[None, :]
    dist = i + pad - j
    q_chunk, k_chunk = i // CHUNK, j // CHUNK
    in_band = (k_chunk >= q_chunk) & (k_chunk <= q_chunk + n_prev)
    first_col = pad - Q_TILE * np.arange(pad // Q_TILE + 1)
    visible = in_band[None] & (j[None] >= first_col[:, None, None])
    return dist, visible


def _toeplitz(g, n_rows, n_cols):
    length = n_rows + n_cols - 1
    skew = jnp.tile(g, (1, n_rows + 1))[:, :n_rows * (length + 1)].reshape(-1, n_rows, length + 1)
    return skew[:, :, :n_cols][:, :, ::-1]


def _alibi_slopes(n_heads):
    return np.array([2.0 ** (-8.0 * (h + 1) / n_heads) for h in range(n_heads)], dtype=np.float32)


def kernel(x, p, ffn1_norm, ffn1_w_gu, ffn1_w_down, mix_norm, w_in, a_q_norm, a_k_norm, a_rel_bias, b_q_norm, b_k_norm, b_sinks, w_gate, w_proj_a, w_proj_b, w_out, ffn2_norm, ffn2_w_gu, ffn2_w_down, ple_norm, w_ple_gate, w_ple_proj):
    b, s, d = x.shape
    depth = ffn1_w_gu.shape[0]
    assert d == D_MODEL and s % Q_STEP == 0 and (b * s) % ROW_TILE == 0
    n_rows = b * s
    q_scale = np.float32(HEAD_DIM ** -0.5 * LOG2_E)

    win_a = Q_TILE + A_PREV_CHUNKS * CHUNK
    _, visible_a = _band_geometry(A_PREV_CHUNKS)
    diag = np.arange(Q_TILE + win_a - 1) - (win_a - 1) + A_PREV_CHUNKS * CHUNK
    rel_idx = np.clip(diag, -A_MAX_REL, A_MAX_REL) + A_MAX_REL
    dist_b, visible_b = _band_geometry(B_PREV_CHUNKS)
    b_lane_heads = np.arange(B_Q_HEADS).reshape(B_KV_HEADS, B_GROUP).T.reshape(-1)
    alibi = -_alibi_slopes(B_Q_HEADS)[:, None, None] * np.abs(dist_b).astype(np.float32)[None]
    bias_b = jnp.asarray(np.where(visible_b[:, None], (alibi * np.float32(LOG2_E))[b_lane_heads][None],
                                  np.float32(NEG_INF)))
    pool = jnp.asarray(np.kron(np.eye(NORM_SLAB // HEAD_DIM), np.full((HEAD_DIM, HEAD_DIM), 1.0 / HEAD_DIM)),
                       dtype=BF16)

    row_f32 = jax.ShapeDtypeStruct((n_rows, D_MODEL), F32)
    act_scratch = pltpu.VMEM((ROW_TILE, D_FF), BF16)

    h = x.reshape(n_rows, D_MODEL)
    for i in range(depth):
        vec = lambda g: g[i].astype(F32).reshape(1, -1)
        qk_gain = jnp.concatenate([
            jnp.tile(a_q_norm[i], A_HEADS) * q_scale, jnp.tile(a_k_norm[i], A_HEADS),
            jnp.ones((A_WIDTH,), F32),
            jnp.tile(b_q_norm[i], B_Q_HEADS) * q_scale, jnp.tile(b_k_norm[i], B_KV_HEADS),
            jnp.ones((B_KV_WIDTH,), F32)]).astype(F32).reshape(1, IN_COLS)
        qb_lo = 3 * A_WIDTH
        w_qb = w_in[i][:, qb_lo:qb_lo + B_Q_WIDTH].reshape(D_MODEL, B_KV_HEADS, B_GROUP, HEAD_DIM)
        w_in_i = jnp.concatenate([w_in[i][:, :qb_lo], w_qb.transpose(0, 2, 1, 3).reshape(D_MODEL, B_Q_WIDTH),
                                  w_in[i][:, qb_lo + B_Q_WIDTH:]], axis=1)
        w_proj_b_i = (w_proj_b[i].reshape(B_KV_HEADS, B_GROUP, HEAD_DIM, D_MODEL)
                      .transpose(1, 0, 2, 3).reshape(B_Q_WIDTH, D_MODEL))

        h1, qkv = _dense_call(
            _ffn1_proj_kernel, n_rows,
            in_specs=[_row_tiled(D_MODEL), _resident((1, D_MODEL)), _resident((D_MODEL, 2 * D_FF)),
                      _resident((D_FF, D_MODEL)), _resident((1, D_MODEL)), _resident((D_MODEL, IN_COLS)),
                      _resident((NORM_SLAB, NORM_SLAB)), _resident((1, IN_COLS))],
            out_specs=[_row_tiled(D_MODEL), _row_tiled(IN_COLS)],
            out_shape=[row_f32, jax.ShapeDtypeStruct((n_rows, IN_COLS), BF16)],
            scratch_shapes=[act_scratch, pltpu.VMEM((ROW_TILE, IN_COLS), F32)], name="ffn1_proj",
        )(h, vec(ffn1_norm), ffn1_w_gu[i].astype(BF16), ffn1_w_down[i].astype(BF16),
          vec(mix_norm), w_in_i.astype(BF16), pool, qk_gain)

        qkv3 = qkv.reshape(b, s, IN_COLS)
        rel = _toeplitz((a_rel_bias[i].astype(F32) * np.float32(LOG2_E))[:, rel_idx], Q_TILE, win_a)
        bias_a = jnp.where(visible_a[:, None], rel[None], NEG_INF)
        ya = _band_attention(qkv3, bias_a, None, q_col=0, k_col=1, v_col=2,
                             n_heads=A_HEADS, n_kv=A_HEADS, n_prev=A_PREV_CHUNKS, name="mixer_a")
        kb_col = (3 * A_WIDTH + B_Q_WIDTH) // B_KV_WIDTH
        sinks = (b_sinks[i].astype(F32) * np.float32(LOG2_E))[b_lane_heads]
        yb = _band_attention(qkv3, bias_b, sinks, q_col=3, k_col=kb_col, v_col=kb_col + 1,
                             n_heads=B_Q_HEADS, n_kv=B_KV_HEADS, n_prev=B_PREV_CHUNKS, name="mixer_b")

        h2 = _dense_call(
            _merge_kernel, n_rows,
            in_specs=[_row_tiled(D_MODEL), _row_tiled(A_WIDTH), _row_tiled(B_Q_WIDTH), _resident((1, D_MODEL)),
                      _resident((D_MODEL, 2 * D_MODEL)), _resident((A_WIDTH, D_MODEL)),
                      _resident((B_Q_WIDTH, D_MODEL)), _resident((D_MODEL, D_MODEL))],
            out_specs=_row_tiled(D_MODEL), out_shape=row_f32, scratch_shapes=[], name="merge",
        )(h1, ya.reshape(n_rows, A_WIDTH), yb.reshape(n_rows, B_Q_WIDTH), vec(mix_norm),
          w_gate[i].astype(BF16), w_proj_a[i].astype(BF16), w_proj_b_i.astype(BF16), w_out[i].astype(BF16))

        h = _dense_call(
            _ffn2_ple_kernel, n_rows,
            in_specs=[_row_tiled(D_MODEL), _row_tiled(PLE_DIM), _resident((1, D_MODEL)),
                      _resident((D_MODEL, 2 * D_FF)), _resident((D_FF, D_MODEL)), _resident((1, D_MODEL)),
                      _resident((D_MODEL, D_MODEL)), _resident((PLE_DIM, D_MODEL))],
            out_specs=_row_tiled(D_MODEL), out_shape=row_f32, scratch_shapes=[act_scratch], name="ffn2_ple",
        )(h2, p[i].reshape(n_rows, PLE_DIM), vec(ffn2_norm), ffn2_w_gu[i].astype(BF16),
          ffn2_w_down[i].astype(BF16), vec(ple_norm), w_ple_gate[i].astype(BF16), w_ple_proj[i].astype(BF16))
    return h.reshape(b, s, D_MODEL)
```

```python
import functools

import numpy as np
import jax
import jax.numpy as jnp
from jax import lax
from jax.experimental import pallas as pl
from jax.experimental.pallas import tpu as pltpu

D_MODEL = 1024
D_FF = 2816
PLE_DIM = 256
CHUNK = 64
HEAD_DIM = 64
A_HEADS = 8
A_PREV_CHUNKS = 8
A_MAX_REL = 128
B_Q_HEADS = 8
B_KV_HEADS = 2
B_GROUP = B_Q_HEADS // B_KV_HEADS
B_PREV_CHUNKS = 2
A_WIDTH = A_HEADS * HEAD_DIM
B_Q_WIDTH = B_Q_HEADS * HEAD_DIM
B_KV_WIDTH = B_KV_HEADS * HEAD_DIM
IN_COLS = 3 * A_WIDTH + B_Q_WIDTH + 2 * B_KV_WIDTH
EPS = 1e-6
NEG_INF = -1e30
LOG2_E = 1.4426950408889634

V7X_MXU_DIM = 256
V7X_LANES = 128
V7X_VMEM_BYTES = 64 * 1024 * 1024
VMEM_LIMIT_BYTES = V7X_VMEM_BYTES - 8 * 1024 * 1024

ROW_TILE = 512
FF_CHUNK = V7X_MXU_DIM
Q_TILE = 2 * CHUNK
Q_STEP = 4 * Q_TILE
NORM_SLAB = V7X_MXU_DIM
SCORE_LOOKAHEAD = 1

BF16 = jnp.bfloat16
F32 = jnp.float32


def _dot(a, b):
    return jnp.dot(a, b, preferred_element_type=F32)


def _sigmoid(x):
    return 1.0 / (1.0 + jnp.exp(-x))


def _rms_norm(h, gain):
    ms = jnp.mean(h * h, axis=-1, keepdims=True)
    return (h * lax.rsqrt(ms + EPS)) * gain


def _swiglu(xn, wgu_ref, wd_ref, act_ref):
    for j in range(D_FF // FF_CHUNK):
        lo = j * FF_CHUNK
        g = _dot(xn, wgu_ref[:, lo:lo + FF_CHUNK])
        u = _dot(xn, wgu_ref[:, D_FF + lo:D_FF + lo + FF_CHUNK])
        act_ref[:, lo:lo + FF_CHUNK] = ((g * _sigmoid(g)) * u).astype(BF16)
    return _dot(act_ref[...], wd_ref[...])


def _head_group_norm(x, pool, gain):
    ms = _dot((x * x).astype(BF16), pool)
    return (x * lax.rsqrt(ms + EPS)) * gain


def _proj_slabs():
    slabs = []
    col = 0
    for width, normed in ((A_WIDTH, True), (A_WIDTH, True), (A_WIDTH, False),
                          (B_Q_WIDTH, True), (B_KV_WIDTH, True), (B_KV_WIDTH, False)):
        for off in range(0, width, NORM_SLAB):
            slabs.append((col + off, min(NORM_SLAB, width - off), normed))
        col += width
    return tuple(slabs)


def _ffn1_proj_kernel(x_ref, g1_ref, wgu_ref, wd_ref, gm_ref, win_ref, pool_ref, qkg_ref,
                      h1_ref, qkv_ref, act_ref, y_ref):
    x = x_ref[...]
    h1 = x + 0.5 * _swiglu(_rms_norm(x, g1_ref[...]).astype(BF16), wgu_ref, wd_ref, act_ref)
    h1_ref[...] = h1
    y_ref[...] = _dot(_rms_norm(h1, gm_ref[...]).astype(BF16), win_ref[...])
    for lo, w, normed in _proj_slabs():
        y = y_ref[:, lo:lo + w]
        if normed:
            y = _head_group_norm(y, pool_ref[:w, :w], qkg_ref[:, lo:lo + w])
        qkv_ref[:, lo:lo + w] = y.astype(BF16)


def _merge_ffn2_ple_kernel(h1_ref, ya_ref, yb_ref, p_ref, gm_ref, wg_ref, wa_ref, wb_ref, wo_ref,
                           g2_ref, wgu_ref, wd_ref, gp_ref, wpg_ref, wpe_ref, out_ref, act_ref):
    h1 = h1_ref[...]
    u = _rms_norm(h1, gm_ref[...]).astype(BF16)
    ga = _sigmoid(_dot(u, wg_ref[:, :D_MODEL]))
    gb = _sigmoid(_dot(u, wg_ref[:, D_MODEL:]))
    merged = ga * _dot(ya_ref[...], wa_ref[...]) + gb * _dot(yb_ref[...], wb_ref[...])
    h2 = h1 + _dot(merged.astype(BF16), wo_ref[...])
    h3 = h2 + 0.5 * _swiglu(_rms_norm(h2, g2_ref[...]).astype(BF16), wgu_ref, wd_ref, act_ref)
    gate = _sigmoid(_dot(_rms_norm(h3, gp_ref[...]).astype(BF16), wpg_ref[...]))
    out_ref[...] = h3 + gate * _dot(p_ref[...].astype(BF16), wpe_ref[...])


def _band_attention_kernel(*refs, n_pairs, kv_pairs, pad, has_sink):
    if has_sink:
        sink_ref, q_ref, k_ref, v_ref, bias_ref, o_ref, kpad_ref, vext_ref = refs
    else:
        q_ref, k_ref, v_ref, bias_ref, o_ref, kpad_ref, vext_ref = refs
    t = pl.program_id(1)
    window = Q_TILE + pad
    seq = k_ref.shape[1]
    pair = 2 * HEAD_DIM

    @pl.when(t == 0)
    def _():
        kpad_ref[:pad, :] = jnp.zeros((pad, kpad_ref.shape[1]), BF16)
        kpad_ref[pad:, :] = k_ref[0]
        for kp in range(kv_pairs):
            vext_ref[:pad, 2 * pair * kp:2 * pair * kp + pair] = jnp.zeros((pad, pair), BF16)
            vext_ref[pad:, 2 * pair * kp:2 * pair * kp + pair] = v_ref[0, :, pair * kp:pair * (kp + 1)]
            vext_ref[:, 2 * pair * kp + pair:2 * pair * (kp + 1)] = jnp.ones((seq + pad, pair), BF16)

    first_head = lax.broadcasted_iota(jnp.int32, (Q_TILE, pair), 1) < HEAD_DIM
    first_rows = lax.broadcasted_iota(jnp.int32, (2 * Q_TILE, 1), 0) < Q_TILE
    tiles_per_step = Q_STEP // Q_TILE

    def window_start(sub):
        return pl.multiple_of((t * tiles_per_step + sub) * Q_TILE, Q_TILE)

    def scores(sub, pp):
        kp = pp % kv_pairs
        variant = jnp.minimum(t * tiles_per_step + sub, bias_ref.shape[0] - 1)
        q = q_ref[0, sub * Q_TILE:(sub + 1) * Q_TILE, pair * pp:pair * (pp + 1)].astype(F32)
        q2 = jnp.concatenate([jnp.where(first_head, q, 0.0), jnp.where(first_head, 0.0, q)], axis=0)
        k = kpad_ref[pl.ds(window_start(sub), window), pair * kp:pair * (kp + 1)]
        s = lax.dot_general(q2.astype(BF16), k, (((1,), (1,)), ((), ())), preferred_element_type=F32)
        return s + bias_ref[variant, 2 * pp:2 * pp + 2].reshape(2 * Q_TILE, window)

    def finish(sub, pp, s):
        kp = pp % kv_pairs
        v = vext_ref[pl.ds(window_start(sub), window), 2 * pair * kp:2 * pair * (kp + 1)]
        m = jnp.max(s, axis=-1, keepdims=True)
        if has_sink:
            sink = jnp.where(first_rows, sink_ref[2 * pp], sink_ref[2 * pp + 1])
            m = jnp.maximum(m, sink)
        o2 = _dot(jnp.exp2(s - m).astype(BF16), v)
        denom = o2[:, pair:]
        if has_sink:
            denom = denom + jnp.exp2(sink - m)
        o2 = o2[:, :pair] / denom
        o_ref[0, sub * Q_TILE:(sub + 1) * Q_TILE, pair * pp:pair * (pp + 1)] = (
            jnp.where(first_head, o2[:Q_TILE], o2[Q_TILE:]).astype(BF16))

    units = [(sub, pp) for sub in range(tiles_per_step) for pp in range(n_pairs)]
    pending = [scores(*u) for u in units[:SCORE_LOOKAHEAD]]
    for n, u in enumerate(units):
        if n + SCORE_LOOKAHEAD < len(units):
            pending.append(scores(*units[n + SCORE_LOOKAHEAD]))
        finish(*u, pending.pop(0))


def _resident(shape):
    return pl.BlockSpec(shape, lambda *_: (0,) * len(shape), pipeline_mode=pl.Buffered(1))


def _row_tiled(width):
    return pl.BlockSpec((ROW_TILE, width), lambda i: (i, 0))


def _dense_call(kernel, n_rows, in_specs, out_specs, out_shape, scratch_shapes, name):
    return pl.pallas_call(
        kernel,
        grid=(n_rows // ROW_TILE,),
        in_specs=in_specs,
        out_specs=out_specs,
        out_shape=out_shape,
        scratch_shapes=scratch_shapes,
        compiler_params=pltpu.CompilerParams(
            dimension_semantics=("arbitrary",), vmem_limit_bytes=VMEM_LIMIT_BYTES),
        name=name,
    )


def _band_attention(qkv, bias, sinks, *, q_col, k_col, v_col, n_heads, n_kv, n_prev, name):
    b, s, _ = qkv.shape
    pad = n_prev * CHUNK
    qw, kw = n_heads * HEAD_DIM, n_kv * HEAD_DIM
    has_sink = sinks is not None
    in_specs = [
        pl.BlockSpec((1, Q_STEP, qw), lambda bi, ti: (bi, ti, q_col)),
        pl.BlockSpec((1, s, kw), lambda bi, ti: (bi, 0, k_col)),
        pl.BlockSpec((1, s, kw), lambda bi, ti: (bi, 0, v_col)),
        _resident(bias.shape),
    ]
    args = [qkv, qkv, qkv, bias]
    if has_sink:
        in_specs.insert(0, pl.BlockSpec(memory_space=pltpu.SMEM))
        args.insert(0, sinks)
    return pl.pallas_call(
        functools.partial(_band_attention_kernel, n_pairs=n_heads // 2, kv_pairs=n_kv // 2,
                          pad=pad, has_sink=has_sink),
        grid=(b, s // Q_STEP),
        in_specs=in_specs,
        out_specs=pl.BlockSpec((1, Q_STEP, qw), lambda bi, ti: (bi, ti, 0)),
        out_shape=jax.ShapeDtypeStruct((b, s, qw), BF16),
        scratch_shapes=[pltpu.VMEM((s + pad, kw), BF16), pltpu.VMEM((s + pad, 2 * kw), BF16)],
        compiler_params=pltpu.CompilerParams(
            dimension_semantics=("arbitrary", "arbitrary"), vmem_limit_bytes=VMEM_LIMIT_BYTES),
        name=name,
    )(*args)


def _band_geometry(n_prev):
    pad = n_prev * CHUNK
    assert pad % Q_TILE == 0
    i = np.arange(Q_TILE)[:, None]
    j = np.arange(Q_TILE + pad)[None, :]
    dist = i + pad - j
    q_chunk, k_chunk = i // CHUNK, j // CHUNK
    in_band = (k_chunk >= q_chunk) & (k_chunk <= q_chunk + n_prev)
    first_col = pad - Q_TILE * np.arange(pad // Q_TILE + 1)
    visible = in_band[None] & (j[None] >= first_col[:, None, None])
    return dist, visible


def _toeplitz(g, n_rows, n_cols):
    length = n_rows + n_cols - 1
    skew = jnp.tile(g, (1, n_rows + 1))[:, :n_rows * (length + 1)].reshape(-1, n_rows, length + 1)
    return skew[:, :, :n_cols][:, :, ::-1]


def _alibi_slopes(n_heads):
    return np.array([2.0 ** (-8.0 * (h + 1) / n_heads) for h in range(n_heads)], dtype=np.float32)


def kernel(x, p, ffn1_norm, ffn1_w_gu, ffn1_w_down, mix_norm, w_in, a_q_norm, a_k_norm, a_rel_bias, b_q_norm, b_k_norm, b_sinks, w_gate, w_proj_a, w_proj_b, w_out, ffn2_norm, ffn2_w_gu, ffn2_w_down, ple_norm, w_ple_gate, w_ple_proj):
    b, s, d = x.shape
    depth = ffn1_w_gu.shape[0]
    assert d == D_MODEL and s % Q_STEP == 0 and (b * s) % ROW_TILE == 0
    n_rows = b * s
    q_scale = np.float32(HEAD_DIM ** -0.5 * LOG2_E)

    win_a = Q_TILE + A_PREV_CHUNKS * CHUNK
    _, visible_a = _band_geometry(A_PREV_CHUNKS)
    diag = np.arange(Q_TILE + win_a - 1) - (win_a - 1) + A_PREV_CHUNKS * CHUNK
    rel_idx = np.clip(diag, -A_MAX_REL, A_MAX_REL) + A_MAX_REL
    dist_b, visible_b = _band_geometry(B_PREV_CHUNKS)
    b_lane_heads = np.arange(B_Q_HEADS).reshape(B_KV_HEADS, B_GROUP).T.reshape(-1)
    alibi = -_alibi_slopes(B_Q_HEADS)[:, None, None] * np.abs(dist_b).astype(np.float32)[None]
    bias_b = jnp.asarray(np.where(visible_b[:, None], (alibi * np.float32(LOG2_E))[b_lane_heads][None],
                                  np.float32(NEG_INF)))
    pool = jnp.asarray(np.kron(np.eye(NORM_SLAB // HEAD_DIM), np.full((HEAD_DIM, HEAD_DIM), 1.0 / HEAD_DIM)),
                       dtype=BF16)

    row_f32 = jax.ShapeDtypeStruct((n_rows, D_MODEL), F32)
    act_scratch = pltpu.VMEM((ROW_TILE, D_FF), BF16)

    h = x.reshape(n_rows, D_MODEL)
    for i in range(depth):
        vec = lambda g: g[i].astype(F32).reshape(1, -1)
        qk_gain = jnp.concatenate([
            jnp.tile(a_q_norm[i], A_HEADS) * q_scale, jnp.tile(a_k_norm[i], A_HEADS),
            jnp.ones((A_WIDTH,), F32),
            jnp.tile(b_q_norm[i], B_Q_HEADS) * q_scale, jnp.tile(b_k_norm[i], B_KV_HEADS),
            jnp.ones((B_KV_WIDTH,), F32)]).astype(F32).reshape(1, IN_COLS)
        qb_lo = 3 * A_WIDTH
        w_qb = w_in[i][:, qb_lo:qb_lo + B_Q_WIDTH].reshape(D_MODEL, B_KV_HEADS, B_GROUP, HEAD_DIM)
        w_in_i = jnp.concatenate([w_in[i][:, :qb_lo], w_qb.transpose(0, 2, 1, 3).reshape(D_MODEL, B_Q_WIDTH),
                                  w_in[i][:, qb_lo + B_Q_WIDTH:]], axis=1)
        w_proj_b_i = (w_proj_b[i].reshape(B_KV_HEADS, B_GROUP, HEAD_DIM, D_MODEL)
                      .transpose(1, 0, 2, 3).reshape(B_Q_WIDTH, D_MODEL))

        h1, qkv = _dense_call(
            _ffn1_proj_kernel, n_rows,
            in_specs=[_row_tiled(D_MODEL), _resident((1, D_MODEL)), _resident((D_MODEL, 2 * D_FF)),
                      _resident((D_FF, D_MODEL)), _resident((1, D_MODEL)), _resident((D_MODEL, IN_COLS)),
                      _resident((NORM_SLAB, NORM_SLAB)), _resident((1, IN_COLS))],
            out_specs=[_row_tiled(D_MODEL), _row_tiled(IN_COLS)],
            out_shape=[row_f32, jax.ShapeDtypeStruct((n_rows, IN_COLS), BF16)],
            scratch_shapes=[act_scratch, pltpu.VMEM((ROW_TILE, IN_COLS), F32)], name="ffn1_proj",
        )(h, vec(ffn1_norm), ffn1_w_gu[i].astype(BF16), ffn1_w_down[i].astype(BF16),
          vec(mix_norm), w_in_i.astype(BF16), pool, qk_gain)

        qkv3 = qkv.reshape(b, s, IN_COLS)
        rel = _toeplitz((a_rel_bias[i].astype(F32) * np.float32(LOG2_E))[:, rel_idx], Q_TILE, win_a)
        bias_a = jnp.where(visible_a[:, None], rel[None], NEG_INF)
        ya = _band_attention(qkv3, bias_a, None, q_col=0, k_col=1, v_col=2,
                             n_heads=A_HEADS, n_kv=A_HEADS, n_prev=A_PREV_CHUNKS, name="mixer_a")
        kb_col = (3 * A_WIDTH + B_Q_WIDTH) // B_KV_WIDTH
        sinks = (b_sinks[i].astype(F32) * np.float32(LOG2_E))[b_lane_heads]
        yb = _band_attention(qkv3, bias_b, sinks, q_col=3, k_col=kb_col, v_col=kb_col + 1,
                             n_heads=B_Q_HEADS, n_kv=B_KV_HEADS, n_prev=B_PREV_CHUNKS, name="mixer_b")

        h = _dense_call(
            _merge_ffn2_ple_kernel, n_rows,
            in_specs=[_row_tiled(D_MODEL), _row_tiled(A_WIDTH), _row_tiled(B_Q_WIDTH), _row_tiled(PLE_DIM),
                      _resident((1, D_MODEL)), _resident((D_MODEL, 2 * D_MODEL)), _resident((A_WIDTH, D_MODEL)),
                      _resident((B_Q_WIDTH, D_MODEL)), _resident((D_MODEL, D_MODEL)),
                      _resident((1, D_MODEL)), _resident((D_MODEL, 2 * D_FF)), _resident((D_FF, D_MODEL)),
                      _resident((1, D_MODEL)), _resident((D_MODEL, D_MODEL)), _resident((PLE_DIM, D_MODEL))],
            out_specs=_row_tiled(D_MODEL), out_shape=row_f32, scratch_shapes=[act_scratch],
            name="merge_ffn2_ple",
        )(h1, ya.reshape(n_rows, A_WIDTH), yb.reshape(n_rows, B_Q_WIDTH), p[i].reshape(n_rows, PLE_DIM),
          vec(mix_norm), w_gate[i].astype(BF16), w_proj_a[i].astype(BF16), w_proj_b_i.astype(BF16),
          w_out[i].astype(BF16), vec(ffn2_norm), ffn2_w_gu[i].astype(BF16), ffn2_w_down[i].astype(BF16),
          vec(ple_norm), w_ple_gate[i].astype(BF16), w_ple_proj[i].astype(BF16))
    return h.reshape(b, s, D_MODEL)
```

```python
import functools

import numpy as np
import jax
import jax.numpy as jnp
from jax import lax
from jax.experimental import pallas as pl
from jax.experimental.pallas import tpu as pltpu

D_MODEL = 1024
D_FF = 2816
PLE_DIM = 256
CHUNK = 64
HEAD_DIM = 64
A_HEADS = 8
A_PREV_CHUNKS = 8
A_MAX_REL = 128
B_Q_HEADS = 8
B_KV_HEADS = 2
B_GROUP = B_Q_HEADS // B_KV_HEADS
B_PREV_CHUNKS = 2
A_WIDTH = A_HEADS * HEAD_DIM
B_Q_WIDTH = B_Q_HEADS * HEAD_DIM
B_KV_WIDTH = B_KV_HEADS * HEAD_DIM
IN_COLS = 3 * A_WIDTH + B_Q_WIDTH + 2 * B_KV_WIDTH
EPS = 1e-6
NEG_INF = -1e30
LOG2_E = 1.4426950408889634

V7X_MXU_DIM = 256
V7X_LANES = 128
V7X_VMEM_BYTES = 64 * 1024 * 1024
VMEM_LIMIT_BYTES = V7X_VMEM_BYTES - 8 * 1024 * 1024

ROW_TILE = 512
FF_CHUNK = V7X_MXU_DIM
Q_TILE = 2 * CHUNK
Q_STEP = 4 * Q_TILE
SCORE_LOOKAHEAD = 1

BF16 = jnp.bfloat16
F32 = jnp.float32


def _dot(a, b):
    return jnp.dot(a, b, preferred_element_type=F32)


def _sigmoid(x):
    return 1.0 / (1.0 + jnp.exp(-x))


def _rms_norm(h, gain):
    ms = jnp.mean(h * h, axis=-1, keepdims=True)
    return (h * lax.rsqrt(ms + EPS)) * gain


def _swiglu(xn, wgu_ref, wd_ref, act_ref):
    for j in range(D_FF // FF_CHUNK):
        lo = j * FF_CHUNK
        g = _dot(xn, wgu_ref[:, lo:lo + FF_CHUNK])
        u = _dot(xn, wgu_ref[:, D_FF + lo:D_FF + lo + FF_CHUNK])
        act_ref[:, lo:lo + FF_CHUNK] = ((g * _sigmoid(g)) * u).astype(BF16)
    return _dot(act_ref[...], wd_ref[...])


def _head_pair_norm(x, gain):
    first_head = lax.broadcasted_iota(jnp.int32, x.shape, 1) < HEAD_DIM
    sq = x * x
    s0 = jnp.sum(jnp.where(first_head, sq, 0.0), axis=-1, keepdims=True)
    s1 = jnp.sum(jnp.where(first_head, 0.0, sq), axis=-1, keepdims=True)
    ms = jnp.where(first_head, s0, s1) * (1.0 / HEAD_DIM)
    return (x * lax.rsqrt(ms + EPS)) * gain


_QKA_COLS = (0, 2 * A_WIDTH)
_VA_COLS = (2 * A_WIDTH, 3 * A_WIDTH)
_QB_COLS = (3 * A_WIDTH, 3 * A_WIDTH + B_Q_WIDTH)
_KB_COLS = (_QB_COLS[1], _QB_COLS[1] + B_KV_WIDTH)
_KVB_COLS = (_QB_COLS[1], IN_COLS)


def _ffn1_proj_kernel(x_ref, g1_ref, wgu_ref, wd_ref, gm_ref, win_ref, qkg_ref,
                      h1_ref, qkv_ref, act_ref, y_ref):
    x = x_ref[...]
    h1 = x + 0.5 * _swiglu(_rms_norm(x, g1_ref[...]).astype(BF16), wgu_ref, wd_ref, act_ref)
    h1_ref[...] = h1
    u = _rms_norm(h1, gm_ref[...]).astype(BF16)
    for lo, hi in (_QKA_COLS, _QB_COLS):
        y_ref[:, lo:hi] = _dot(u, win_ref[:, lo:hi])
    half = ROW_TILE // 2
    lo, hi = _KVB_COLS
    y_ref[:half, lo:hi] = _dot(u[:half], win_ref[:, lo:hi])
    y_ref[half:, lo:hi] = _dot(u[half:], win_ref[:, lo:hi])
    lo, hi = _VA_COLS
    y_ref[:, lo:hi] = _dot(u, win_ref[:, lo:hi])

    for first, last in (_QKA_COLS, _QB_COLS, _KB_COLS):
        for lo in range(first, last, V7X_LANES):
            cols = slice(lo, lo + V7X_LANES)
            qkv_ref[:, cols] = _head_pair_norm(y_ref[:, cols], qkg_ref[:, cols]).astype(BF16)
    for lo, hi in (_VA_COLS, (_KB_COLS[1], IN_COLS)):
        qkv_ref[:, lo:hi] = y_ref[:, lo:hi].astype(BF16)


def _merge_ffn2_ple_kernel(h1_ref, ya_ref, yb_ref, p_ref, gm_ref, wg_ref, wa_ref, wb_ref, wo_ref,
                           g2_ref, wgu_ref, wd_ref, gp_ref, wpg_ref, wpe_ref, out_ref, act_ref):
    h1 = h1_ref[...]
    u = _rms_norm(h1, gm_ref[...]).astype(BF16)
    ga = _sigmoid(_dot(u, wg_ref[:, :D_MODEL]))
    gb = _sigmoid(_dot(u, wg_ref[:, D_MODEL:]))
    merged = ga * _dot(ya_ref[...], wa_ref[...]) + gb * _dot(yb_ref[...], wb_ref[...])
    h2 = h1 + _dot(merged.astype(BF16), wo_ref[...])
    h3 = h2 + 0.5 * _swiglu(_rms_norm(h2, g2_ref[...]).astype(BF16), wgu_ref, wd_ref, act_ref)
    gate = _sigmoid(_dot(_rms_norm(h3, gp_ref[...]).astype(BF16), wpg_ref[...]))
    out_ref[...] = h3 + gate * _dot(p_ref[...].astype(BF16), wpe_ref[...])


def _band_attention_kernel(*refs, n_pairs, kv_pairs, pad, has_sink):
    if has_sink:
        sink_ref, q_ref, k_ref, v_ref, bias_ref, o_ref, kpad_ref, vext_ref = refs
    else:
        q_ref, k_ref, v_ref, bias_ref, o_ref, kpad_ref, vext_ref = refs
    t = pl.program_id(1)
    window = Q_TILE + pad
    seq = k_ref.shape[1]
    pair = 2 * HEAD_DIM

    @pl.when(t == 0)
    def _():
        kpad_ref[:pad, :] = jnp.zeros((pad, kpad_ref.shape[1]), BF16)
        kpad_ref[pad:, :] = k_ref[0]
        for kp in range(kv_pairs):
            vext_ref[:pad, 2 * pair * kp:2 * pair * kp + pair] = jnp.zeros((pad, pair), BF16)
            vext_ref[pad:, 2 * pair * kp:2 * pair * kp + pair] = v_ref[0, :, pair * kp:pair * (kp + 1)]
            vext_ref[:, 2 * pair * kp + pair:2 * pair * (kp + 1)] = jnp.ones((seq + pad, pair), BF16)

    first_head = lax.broadcasted_iota(jnp.int32, (Q_TILE, pair), 1) < HEAD_DIM
    first_rows = lax.broadcasted_iota(jnp.int32, (2 * Q_TILE, 1), 0) < Q_TILE
    tiles_per_step = Q_STEP // Q_TILE

    def window_start(sub):
        return pl.multiple_of((t * tiles_per_step + sub) * Q_TILE, Q_TILE)

    def scores(sub, pp):
        kp = pp % kv_pairs
        variant = jnp.minimum(t * tiles_per_step + sub, bias_ref.shape[0] - 1)
        q = q_ref[0, sub * Q_TILE:(sub + 1) * Q_TILE, pair * pp:pair * (pp + 1)].astype(F32)
        q2 = jnp.concatenate([jnp.where(first_head, q, 0.0), jnp.where(first_head, 0.0, q)], axis=0)
        k = kpad_ref[pl.ds(window_start(sub), window), pair * kp:pair * (kp + 1)]
        s = lax.dot_general(q2.astype(BF16), k, (((1,), (1,)), ((), ())), preferred_element_type=F32)
        return s + bias_ref[variant, 2 * pp:2 * pp + 2].reshape(2 * Q_TILE, window)

    def finish(sub, pp, s):
        kp = pp % kv_pairs
        v = vext_ref[pl.ds(window_start(sub), window), 2 * pair * kp:2 * pair * (kp + 1)]
        m = jnp.max(s, axis=-1, keepdims=True)
        if has_sink:
            sink = jnp.where(first_rows, sink_ref[2 * pp], sink_ref[2 * pp + 1])
            m = jnp.maximum(m, sink)
        o2 = _dot(jnp.exp2(s - m).astype(BF16), v)
        denom = o2[:, pair:]
        if has_sink:
            denom = denom + jnp.exp2(sink - m)
        o2 = o2[:, :pair] / denom
        o_ref[0, sub * Q_TILE:(sub + 1) * Q_TILE, pair * pp:pair * (pp + 1)] = (
            jnp.where(first_head, o2[:Q_TILE], o2[Q_TILE:]).astype(BF16))

    units = [(sub, pp) for sub in range(tiles_per_step) for pp in range(n_pairs)]
    pending = [scores(*u) for u in units[:SCORE_LOOKAHEAD]]
    for n, u in enumerate(units):
        if n + SCORE_LOOKAHEAD < len(units):
            pending.append(scores(*units[n + SCORE_LOOKAHEAD]))
        finish(*u, pending.pop(0))


def _resident(shape):
    return pl.BlockSpec(shape, lambda *_: (0,) * len(shape), pipeline_mode=pl.Buffered(1))


def _row_tiled(width):
    return pl.BlockSpec((ROW_TILE, width), lambda i: (i, 0))


def _dense_call(kernel, n_rows, in_specs, out_specs, out_shape, scratch_shapes, name):
    return pl.pallas_call(
        kernel,
        grid=(n_rows // ROW_TILE,),
        in_specs=in_specs,
        out_specs=out_specs,
        out_shape=out_shape,
        scratch_shapes=scratch_shapes,
        compiler_params=pltpu.CompilerParams(
            dimension_semantics=("arbitrary",), vmem_limit_bytes=VMEM_LIMIT_BYTES),
        name=name,
    )


def _band_attention(qkv, bias, sinks, *, q_col, k_col, v_col, n_heads, n_kv, n_prev, name):
    b, s, _ = qkv.shape
    pad = n_prev * CHUNK
    qw, kw = n_heads * HEAD_DIM, n_kv * HEAD_DIM
    has_sink = sinks is not None
    in_specs = [
        pl.BlockSpec((1, Q_STEP, qw), lambda bi, ti: (bi, ti, q_col)),
        pl.BlockSpec((1, s, kw), lambda bi, ti: (bi, 0, k_col)),
        pl.BlockSpec((1, s, kw), lambda bi, ti: (bi, 0, v_col)),
        _resident(bias.shape),
    ]
    args = [qkv, qkv, qkv, bias]
    if has_sink:
        in_specs.insert(0, pl.BlockSpec(memory_space=pltpu.SMEM))
        args.insert(0, sinks)
    return pl.pallas_call(
        functools.partial(_band_attention_kernel, n_pairs=n_heads // 2, kv_pairs=n_kv // 2,
                          pad=pad, has_sink=has_sink),
        grid=(b, s // Q_STEP),
        in_specs=in_specs,
        out_specs=pl.BlockSpec((1, Q_STEP, qw), lambda bi, ti: (bi, ti, 0)),
        out_shape=jax.ShapeDtypeStruct((b, s, qw), BF16),
        scratch_shapes=[pltpu.VMEM((s + pad, kw), BF16), pltpu.VMEM((s + pad, 2 * kw), BF16)],
        compiler_params=pltpu.CompilerParams(
            dimension_semantics=("arbitrary", "arbitrary"), vmem_limit_bytes=VMEM_LIMIT_BYTES),
        name=name,
    )(*args)


def _band_geometry(n_prev):
    pad = n_prev * CHUNK
    assert pad % Q_TILE == 0
    i = np.arange(Q_TILE)[:, None]
    j = np.arange(Q_TILE + pad)[None, :]
    dist = i + pad - j
    q_chunk, k_chunk = i // CHUNK, j // CHUNK
    in_band = (k_chunk >= q_chunk) & (k_chunk <= q_chunk + n_prev)
    first_col = pad - Q_TILE * np.arange(pad // Q_TILE + 1)
    visible = in_band[None] & (j[None] >= first_col[:, None, None])
    return dist, visible


def _toeplitz(g, n_rows, n_cols):
    length = n_rows + n_cols - 1
    skew = jnp.tile(g, (1, n_rows + 1))[:, :n_rows * (length + 1)].reshape(-1, n_rows, length + 1)
    return skew[:, :, :n_cols][:, :, ::-1]


def _alibi_slopes(n_heads):
    return np.array([2.0 ** (-8.0 * (h + 1) / n_heads) for h in range(n_heads)], dtype=np.float32)


def kernel(x, p, ffn1_norm, ffn1_w_gu, ffn1_w_down, mix_norm, w_in, a_q_norm, a_k_norm, a_rel_bias, b_q_norm, b_k_norm, b_sinks, w_gate, w_proj_a, w_proj_b, w_out, ffn2_norm, ffn2_w_gu, ffn2_w_down, ple_norm, w_ple_gate, w_ple_proj):
    b, s, d = x.shape
    depth = ffn1_w_gu.shape[0]
    assert d == D_MODEL and s % Q_STEP == 0 and (b * s) % ROW_TILE == 0
    n_rows = b * s
    q_scale = np.float32(HEAD_DIM ** -0.5 * LOG2_E)

    win_a = Q_TILE + A_PREV_CHUNKS * CHUNK
    _, visible_a = _band_geometry(A_PREV_CHUNKS)
    diag = np.arange(Q_TILE + win_a - 1) - (win_a - 1) + A_PREV_CHUNKS * CHUNK
    rel_idx = np.clip(diag, -A_MAX_REL, A_MAX_REL) + A_MAX_REL
    dist_b, visible_b = _band_geometry(B_PREV_CHUNKS)
    b_lane_heads = np.arange(B_Q_HEADS).reshape(B_KV_HEADS, B_GROUP).T.reshape(-1)
    alibi = -_alibi_slopes(B_Q_HEADS)[:, None, None] * np.abs(dist_b).astype(np.float32)[None]
    bias_b = jnp.asarray(np.where(visible_b[:, None], (alibi * np.float32(LOG2_E))[b_lane_heads][None],
                                  np.float32(NEG_INF)))

    row_f32 = jax.ShapeDtypeStruct((n_rows, D_MODEL), F32)
    act_scratch = pltpu.VMEM((ROW_TILE, D_FF), BF16)

    h = x.reshape(n_rows, D_MODEL)
    for i in range(depth):
        vec = lambda g: g[i].astype(F32).reshape(1, -1)
        qk_gain = jnp.concatenate([
            jnp.tile(a_q_norm[i], A_HEADS) * q_scale, jnp.tile(a_k_norm[i], A_HEADS),
            jnp.ones((A_WIDTH,), F32),
            jnp.tile(b_q_norm[i], B_Q_HEADS) * q_scale, jnp.tile(b_k_norm[i], B_KV_HEADS),
            jnp.ones((B_KV_WIDTH,), F32)]).astype(F32).reshape(1, IN_COLS)
        qb_lo = 3 * A_WIDTH
        w_qb = w_in[i][:, qb_lo:qb_lo + B_Q_WIDTH].reshape(D_MODEL, B_KV_HEADS, B_GROUP, HEAD_DIM)
        w_in_i = jnp.concatenate([w_in[i][:, :qb_lo], w_qb.transpose(0, 2, 1, 3).reshape(D_MODEL, B_Q_WIDTH),
                                  w_in[i][:, qb_lo + B_Q_WIDTH:]], axis=1)
        w_proj_b_i = (w_proj_b[i].reshape(B_KV_HEADS, B_GROUP, HEAD_DIM, D_MODEL)
                      .transpose(1, 0, 2, 3).reshape(B_Q_WIDTH, D_MODEL))

        h1, qkv = _dense_call(
            _ffn1_proj_kernel, n_rows,
            in_specs=[_row_tiled(D_MODEL), _resident((1, D_MODEL)), _resident((D_MODEL, 2 * D_FF)),
                      _resident((D_FF, D_MODEL)), _resident((1, D_MODEL)), _resident((D_MODEL, IN_COLS)),
                      _resident((1, IN_COLS))],
            out_specs=[_row_tiled(D_MODEL), _row_tiled(IN_COLS)],
            out_shape=[row_f32, jax.ShapeDtypeStruct((n_rows, IN_COLS), BF16)],
            scratch_shapes=[act_scratch, pltpu.VMEM((ROW_TILE, IN_COLS), F32)], name="ffn1_proj",
        )(h, vec(ffn1_norm), ffn1_w_gu[i].astype(BF16), ffn1_w_down[i].astype(BF16),
          vec(mix_norm), w_in_i.astype(BF16), qk_gain)

        qkv3 = qkv.reshape(b, s, IN_COLS)
        rel = _toeplitz((a_rel_bias[i].astype(F32) * np.float32(LOG2_E))[:, rel_idx], Q_TILE, win_a)
        bias_a = jnp.where(visible_a[:, None], rel[None], NEG_INF)
        ya = _band_attention(qkv3, bias_a, None, q_col=0, k_col=1, v_col=2,
                             n_heads=A_HEADS, n_kv=A_HEADS, n_prev=A_PREV_CHUNKS, name="mixer_a")
        kb_col = (3 * A_WIDTH + B_Q_WIDTH) // B_KV_WIDTH
        sinks = (b_sinks[i].astype(F32) * np.float32(LOG2_E))[b_lane_heads]
        yb = _band_attention(qkv3, bias_b, sinks, q_col=3, k_col=kb_col, v_col=kb_col + 1,
                             n_heads=B_Q_HEADS, n_kv=B_KV_HEADS, n_prev=B_PREV_CHUNKS, name="mixer_b")

        h = _dense_call(
            _merge_ffn2_ple_kernel, n_rows,
            in_specs=[_row_tiled(D_MODEL), _row_tiled(A_WIDTH), _row_tiled(B_Q_WIDTH), _row_tiled(PLE_DIM),
                      _resident((1, D_MODEL)), _resident((D_MODEL, 2 * D_MODEL)), _resident((A_WIDTH, D_MODEL)),
                      _resident((B_Q_WIDTH, D_MODEL)), _resident((D_MODEL, D_MODEL)),
                      _resident((1, D_MODEL)), _resident((D_MODEL, 2 * D_FF)), _resident((D_FF, D_MODEL)),
                      _resident((1, D_MODEL)), _resident((D_MODEL, D_MODEL)), _resident((PLE_DIM, D_MODEL))],
            out_specs=_row_tiled(D_MODEL), out_shape=row_f32, scratch_shapes=[act_scratch],
            name="merge_ffn2_ple",
        )(h1, ya.reshape(n_rows, A_WIDTH), yb.reshape(n_rows, B_Q_WIDTH), p[i].reshape(n_rows, PLE_DIM),
          vec(mix_norm), w_gate[i].astype(BF16), w_proj_a[i].astype(BF16), w_proj_b_i.astype(BF16),
          w_out[i].astype(BF16), vec(ffn2_norm), ffn2_w_gu[i].astype(BF16), ffn2_w_down[i].astype(BF16),
          vec(ple_norm), w_ple_gate[i].astype(BF16), w_ple_proj[i].astype(BF16))
    return h.reshape(b, s, D_MODEL)
```

```python
import functools

import numpy as np
import jax
import jax.numpy as jnp
from jax import lax
from jax.experimental import pallas as pl
from jax.experimental.pallas import tpu as pltpu

D_MODEL = 1024
D_FF = 2816
PLE_DIM = 256
CHUNK = 64
HEAD_DIM = 64
A_HEADS = 8
A_PREV_CHUNKS = 8
A_MAX_REL = 128
B_Q_HEADS = 8
B_KV_HEADS = 2
B_GROUP = B_Q_HEADS // B_KV_HEADS
B_PREV_CHUNKS = 2
A_WIDTH = A_HEADS * HEAD_DIM
B_Q_WIDTH = B_Q_HEADS * HEAD_DIM
B_KV_WIDTH = B_KV_HEADS * HEAD_DIM
IN_COLS = 3 * A_WIDTH + B_Q_WIDTH + 2 * B_KV_WIDTH
EPS = 1e-6
NEG_INF = -1e30
LOG2_E = 1.4426950408889634

V7X_MXU_DIM = 256
V7X_LANES = 128
V7X_VMEM_BYTES = 64 * 1024 * 1024
VMEM_LIMIT_BYTES = V7X_VMEM_BYTES - 8 * 1024 * 1024

ROW_TILE = 512
FF_CHUNK = V7X_MXU_DIM
UP_STAGE_ROWS = 128
DOWN_STAGE_ROWS = 256
Q_TILE = 2 * CHUNK
Q_STEP = 4 * Q_TILE
SCORE_LOOKAHEAD = 1

BF16 = jnp.bfloat16
F32 = jnp.float32


def _dot(a, b):
    return jnp.dot(a, b, preferred_element_type=F32)


def _sigmoid(x):
    return 1.0 / (1.0 + jnp.exp(-x))


def _rms_norm(h, gain):
    ms = jnp.mean(h * h, axis=-1, keepdims=True)
    return (h * lax.rsqrt(ms + EPS)) * gain


def _swiglu(xn, wgu_ref, wd_ref, act_ref):
    for j in range(D_FF // FF_CHUNK):
        lo = j * FF_CHUNK
        g = _dot(xn, wgu_ref[:, lo:lo + FF_CHUNK])
        u = _dot(xn, wgu_ref[:, D_FF + lo:D_FF + lo + FF_CHUNK])
        act_ref[:, lo:lo + FF_CHUNK] = ((g * _sigmoid(g)) * u).astype(BF16)
    return _dot(act_ref[...], wd_ref[...])


def _fetch_ffn_weights(wgu_hbm, wd_hbm, wgu_ref, wd_ref, stage_gu_ref, stage_d_ref, sem_ref):
    @pl.when(pl.program_id(0) == 0)
    def _():
        for w, (src, dst, stage) in enumerate(((wgu_hbm, wgu_ref, stage_gu_ref),
                                               (wd_hbm, wd_ref, stage_d_ref))):
            rows = stage.shape[1]
            n_chunks = src.shape[0] // rows

            def copy(c, src=src, stage=stage, rows=rows, w=w):
                return pltpu.make_async_copy(src.at[pl.ds(c * rows, rows)], stage.at[c % 2],
                                             sem_ref.at[w, c % 2])

            copy(0).start()
            for c in range(n_chunks):
                if c + 1 < n_chunks:
                    copy(c + 1).start()
                copy(c).wait()
                dst[c * rows:(c + 1) * rows, :] = stage[c % 2].astype(BF16)


def _ffn_weight_scratch():
    return [pltpu.VMEM((D_MODEL, 2 * D_FF), BF16), pltpu.VMEM((D_FF, D_MODEL), BF16),
            pltpu.VMEM((2, UP_STAGE_ROWS, 2 * D_FF), F32), pltpu.VMEM((2, DOWN_STAGE_ROWS, D_MODEL), F32),
            pltpu.SemaphoreType.DMA((2, 2))]


def _head_pair_norm(x, gain):
    first_head = lax.broadcasted_iota(jnp.int32, x.shape, 1) < HEAD_DIM
    sq = x * x
    s0 = jnp.sum(jnp.where(first_head, sq, 0.0), axis=-1, keepdims=True)
    s1 = jnp.sum(jnp.where(first_head, 0.0, sq), axis=-1, keepdims=True)
    ms = jnp.where(first_head, s0, s1) * (1.0 / HEAD_DIM)
    return (x * lax.rsqrt(ms + EPS)) * gain


_QKA_COLS = (0, 2 * A_WIDTH)
_VA_COLS = (2 * A_WIDTH, 3 * A_WIDTH)
_QB_COLS = (3 * A_WIDTH, 3 * A_WIDTH + B_Q_WIDTH)
_KB_COLS = (_QB_COLS[1], _QB_COLS[1] + B_KV_WIDTH)
_KVB_COLS = (_QB_COLS[1], IN_COLS)


def _ffn1_proj_kernel(x_ref, g1_ref, wgu_hbm, wd_hbm, gm_ref, win_ref, qkg_ref,
                      h1_ref, qkv_ref, act_ref, y_ref, wgu_ref, wd_ref, *fetch_scratch):
    _fetch_ffn_weights(wgu_hbm, wd_hbm, wgu_ref, wd_ref, *fetch_scratch)
    x = x_ref[...]
    h1 = x + 0.5 * _swiglu(_rms_norm(x, g1_ref[...]).astype(BF16), wgu_ref, wd_ref, act_ref)
    h1_ref[...] = h1
    u = _rms_norm(h1, gm_ref[...]).astype(BF16)
    for lo, hi in (_QKA_COLS, _QB_COLS):
        y_ref[:, lo:hi] = _dot(u, win_ref[:, lo:hi])
    half = ROW_TILE // 2
    lo, hi = _KVB_COLS
    y_ref[:half, lo:hi] = _dot(u[:half], win_ref[:, lo:hi])
    y_ref[half:, lo:hi] = _dot(u[half:], win_ref[:, lo:hi])
    lo, hi = _VA_COLS
    y_ref[:, lo:hi] = _dot(u, win_ref[:, lo:hi])

    for first, last in (_QKA_COLS, _QB_COLS, _KB_COLS):
        for lo in range(first, last, V7X_LANES):
            cols = slice(lo, lo + V7X_LANES)
            qkv_ref[:, cols] = _head_pair_norm(y_ref[:, cols], qkg_ref[:, cols]).astype(BF16)
    for lo, hi in (_VA_COLS, (_KB_COLS[1], IN_COLS)):
        qkv_ref[:, lo:hi] = y_ref[:, lo:hi].astype(BF16)


def _merge_ffn2_ple_kernel(h1_ref, ya_ref, yb_ref, p_ref, gm_ref, wg_ref, wa_ref, wb_ref, wo_ref,
                           g2_ref, wgu_hbm, wd_hbm, gp_ref, wpg_ref, wpe_ref, out_ref, act_ref,
                           wgu_ref, wd_ref, *fetch_scratch):
    _fetch_ffn_weights(wgu_hbm, wd_hbm, wgu_ref, wd_ref, *fetch_scratch)
    h1 = h1_ref[...]
    u = _rms_norm(h1, gm_ref[...]).astype(BF16)
    ga = _sigmoid(_dot(u, wg_ref[:, :D_MODEL]))
    gb = _sigmoid(_dot(u, wg_ref[:, D_MODEL:]))
    merged = ga * _dot(ya_ref[...], wa_ref[...]) + gb * _dot(yb_ref[...], wb_ref[...])
    h2 = h1 + _dot(merged.astype(BF16), wo_ref[...])
    h3 = h2 + 0.5 * _swiglu(_rms_norm(h2, g2_ref[...]).astype(BF16), wgu_ref, wd_ref, act_ref)
    gate = _sigmoid(_dot(_rms_norm(h3, gp_ref[...]).astype(BF16), wpg_ref[...]))
    out_ref[...] = h3 + gate * _dot(p_ref[...].astype(BF16), wpe_ref[...])


def _band_attention_kernel(*refs, n_pairs, kv_pairs, pad, has_sink):
    if has_sink:
        sink_ref, q_ref, k_ref, v_ref, bias_ref, o_ref, kpad_ref, vext_ref = refs
    else:
        q_ref, k_ref, v_ref, bias_ref, o_ref, kpad_ref, vext_ref = refs
    t = pl.program_id(1)
    window = Q_TILE + pad
    seq = k_ref.shape[1]
    pair = 2 * HEAD_DIM

    @pl.when(t == 0)
    def _():
        kpad_ref[:pad, :] = jnp.zeros((pad, kpad_ref.shape[1]), BF16)
        kpad_ref[pad:, :] = k_ref[0]
        for kp in range(kv_pairs):
            vext_ref[:pad, 2 * pair * kp:2 * pair * kp + pair] = jnp.zeros((pad, pair), BF16)
            vext_ref[pad:, 2 * pair * kp:2 * pair * kp + pair] = v_ref[0, :, pair * kp:pair * (kp + 1)]
            vext_ref[:, 2 * pair * kp + pair:2 * pair * (kp + 1)] = jnp.ones((seq + pad, pair), BF16)

    first_head = lax.broadcasted_iota(jnp.int32, (Q_TILE, pair), 1) < HEAD_DIM
    first_rows = lax.broadcasted_iota(jnp.int32, (2 * Q_TILE, 1), 0) < Q_TILE
    tiles_per_step = Q_STEP // Q_TILE

    def window_start(sub):
        return pl.multiple_of((t * tiles_per_step + sub) * Q_TILE, Q_TILE)

    def scores(sub, pp):
        kp = pp % kv_pairs
        variant = jnp.minimum(t * tiles_per_step + sub, bias_ref.shape[0] - 1)
        q = q_ref[0, sub * Q_TILE:(sub + 1) * Q_TILE, pair * pp:pair * (pp + 1)].astype(F32)
        q2 = jnp.concatenate([jnp.where(first_head, q, 0.0), jnp.where(first_head, 0.0, q)], axis=0)
        k = kpad_ref[pl.ds(window_start(sub), window), pair * kp:pair * (kp + 1)]
        s = lax.dot_general(q2.astype(BF16), k, (((1,), (1,)), ((), ())), preferred_element_type=F32)
        return s + bias_ref[variant, 2 * pp:2 * pp + 2].reshape(2 * Q_TILE, window)

    def finish(sub, pp, s):
        kp = pp % kv_pairs
        v = vext_ref[pl.ds(window_start(sub), window), 2 * pair * kp:2 * pair * (kp + 1)]
        m = jnp.max(s, axis=-1, keepdims=True)
        if has_sink:
            sink = jnp.where(first_rows, sink_ref[2 * pp], sink_ref[2 * pp + 1])
            m = jnp.maximum(m, sink)
        o2 = _dot(jnp.exp2(s - m).astype(BF16), v)
        denom = o2[:, pair:]
        if has_sink:
            denom = denom + jnp.exp2(sink - m)
        o2 = o2[:, :pair] / denom
        o_ref[0, sub * Q_TILE:(sub + 1) * Q_TILE, pair * pp:pair * (pp + 1)] = (
            jnp.where(first_head, o2[:Q_TILE], o2[Q_TILE:]).astype(BF16))

    units = [(sub, pp) for sub in range(tiles_per_step) for pp in range(n_pairs)]
    pending = [scores(*u) for u in units[:SCORE_LOOKAHEAD]]
    for n, u in enumerate(units):
        if n + SCORE_LOOKAHEAD < len(units):
            pending.append(scores(*units[n + SCORE_LOOKAHEAD]))
        finish(*u, pending.pop(0))


def _resident(shape):
    return pl.BlockSpec(shape, lambda *_: (0,) * len(shape), pipeline_mode=pl.Buffered(1))


def _row_tiled(width):
    return pl.BlockSpec((ROW_TILE, width), lambda i: (i, 0))


def _dense_call(kernel, n_rows, in_specs, out_specs, out_shape, scratch_shapes, name):
    return pl.pallas_call(
        kernel,
        grid=(n_rows // ROW_TILE,),
        in_specs=in_specs,
        out_specs=out_specs,
        out_shape=out_shape,
        scratch_shapes=scratch_shapes,
        compiler_params=pltpu.CompilerParams(
            dimension_semantics=("arbitrary",), vmem_limit_bytes=VMEM_LIMIT_BYTES),
        name=name,
    )


def _band_attention(qkv, bias, sinks, *, q_col, k_col, v_col, n_heads, n_kv, n_prev, name):
    b, s, _ = qkv.shape
    pad = n_prev * CHUNK
    qw, kw = n_heads * HEAD_DIM, n_kv * HEAD_DIM
    has_sink = sinks is not None
    in_specs = [
        pl.BlockSpec((1, Q_STEP, qw), lambda bi, ti: (bi, ti, q_col)),
        pl.BlockSpec((1, s, kw), lambda bi, ti: (bi, 0, k_col)),
        pl.BlockSpec((1, s, kw), lambda bi, ti: (bi, 0, v_col)),
        _resident(bias.shape),
    ]
    args = [qkv, qkv, qkv, bias]
    if has_sink:
        in_specs.insert(0, pl.BlockSpec(memory_space=pltpu.SMEM))
        args.insert(0, sinks)
    return pl.pallas_call(
        functools.partial(_band_attention_kernel, n_pairs=n_heads // 2, kv_pairs=n_kv // 2,
                          pad=pad, has_sink=has_sink),
        grid=(b, s // Q_STEP),
        in_specs=in_specs,
        out_specs=pl.BlockSpec((1, Q_STEP, qw), lambda bi, ti: (bi, ti, 0)),
        out_shape=jax.ShapeDtypeStruct((b, s, qw), BF16),
        scratch_shapes=[pltpu.VMEM((s + pad, kw), BF16), pltpu.VMEM((s + pad, 2 * kw), BF16)],
        compiler_params=pltpu.CompilerParams(
            dimension_semantics=("arbitrary", "arbitrary"), vmem_limit_bytes=VMEM_LIMIT_BYTES),
        name=name,
    )(*args)


def _band_geometry(n_prev):
    pad = n_prev * CHUNK
    assert pad % Q_TILE == 0
    i = np.arange(Q_TILE)[:, None]
    j = np.arange(Q_TILE + pad)[None, :]
    dist = i + pad - j
    q_chunk, k_chunk = i // CHUNK, j // CHUNK
    in_band = (k_chunk >= q_chunk) & (k_chunk <= q_chunk + n_prev)
    first_col = pad - Q_TILE * np.arange(pad // Q_TILE + 1)
    visible = in_band[None] & (j[None] >= first_col[:, None, None])
    return dist, visible


def _toeplitz(g, n_rows, n_cols):
    length = n_rows + n_cols - 1
    skew = jnp.tile(g, (1, n_rows + 1))[:, :n_rows * (length + 1)].reshape(-1, n_rows, length + 1)
    return skew[:, :, :n_cols][:, :, ::-1]


def _alibi_slopes(n_heads):
    return np.array([2.0 ** (-8.0 * (h + 1) / n_heads) for h in range(n_heads)], dtype=np.float32)


def kernel(x, p, ffn1_norm, ffn1_w_gu, ffn1_w_down, mix_norm, w_in, a_q_norm, a_k_norm, a_rel_bias, b_q_norm, b_k_norm, b_sinks, w_gate, w_proj_a, w_proj_b, w_out, ffn2_norm, ffn2_w_gu, ffn2_w_down, ple_norm, w_ple_gate, w_ple_proj):
    b, s, d = x.shape
    depth = ffn1_w_gu.shape[0]
    assert d == D_MODEL and s % Q_STEP == 0 and (b * s) % ROW_TILE == 0
    n_rows = b * s
    q_scale = np.float32(HEAD_DIM ** -0.5 * LOG2_E)

    win_a = Q_TILE + A_PREV_CHUNKS * CHUNK
    _, visible_a = _band_geometry(A_PREV_CHUNKS)
    diag = np.arange(Q_TILE + win_a - 1) - (win_a - 1) + A_PREV_CHUNKS * CHUNK
    rel_idx = np.clip(diag, -A_MAX_REL, A_MAX_REL) + A_MAX_REL
    dist_b, visible_b = _band_geometry(B_PREV_CHUNKS)
    b_lane_heads = np.arange(B_Q_HEADS).reshape(B_KV_HEADS, B_GROUP).T.reshape(-1)
    alibi = -_alibi_slopes(B_Q_HEADS)[:, None, None] * np.abs(dist_b).astype(np.float32)[None]
    bias_b = jnp.asarray(np.where(visible_b[:, None], (alibi * np.float32(LOG2_E))[b_lane_heads][None],
                                  np.float32(NEG_INF)))

    row_f32 = jax.ShapeDtypeStruct((n_rows, D_MODEL), F32)
    in_hbm = pl.BlockSpec(memory_space=pl.ANY)
    act_scratch = pltpu.VMEM((ROW_TILE, D_FF), BF16)

    h = x.reshape(n_rows, D_MODEL)
    for i in range(depth):
        vec = lambda g: g[i].astype(F32).reshape(1, -1)
        qk_gain = jnp.concatenate([
            jnp.tile(a_q_norm[i], A_HEADS) * q_scale, jnp.tile(a_k_norm[i], A_HEADS),
            jnp.ones((A_WIDTH,), F32),
            jnp.tile(b_q_norm[i], B_Q_HEADS) * q_scale, jnp.tile(b_k_norm[i], B_KV_HEADS),
            jnp.ones((B_KV_WIDTH,), F32)]).astype(F32).reshape(1, IN_COLS)
        qb_lo = 3 * A_WIDTH
        w_qb = w_in[i][:, qb_lo:qb_lo + B_Q_WIDTH].reshape(D_MODEL, B_KV_HEADS, B_GROUP, HEAD_DIM)
        w_in_i = jnp.concatenate([w_in[i][:, :qb_lo], w_qb.transpose(0, 2, 1, 3).reshape(D_MODEL, B_Q_WIDTH),
                                  w_in[i][:, qb_lo + B_Q_WIDTH:]], axis=1)
        w_proj_b_i = (w_proj_b[i].reshape(B_KV_HEADS, B_GROUP, HEAD_DIM, D_MODEL)
                      .transpose(1, 0, 2, 3).reshape(B_Q_WIDTH, D_MODEL))

        h1, qkv = _dense_call(
            _ffn1_proj_kernel, n_rows,
            in_specs=[_row_tiled(D_MODEL), _resident((1, D_MODEL)), in_hbm, in_hbm,
                      _resident((1, D_MODEL)), _resident((D_MODEL, IN_COLS)), _resident((1, IN_COLS))],
            out_specs=[_row_tiled(D_MODEL), _row_tiled(IN_COLS)],
            out_shape=[row_f32, jax.ShapeDtypeStruct((n_rows, IN_COLS), BF16)],
            scratch_shapes=[act_scratch, pltpu.VMEM((ROW_TILE, IN_COLS), F32)] + _ffn_weight_scratch(),
            name="ffn1_proj",
        )(h, vec(ffn1_norm), ffn1_w_gu[i].astype(F32), ffn1_w_down[i].astype(F32),
          vec(mix_norm), w_in_i.astype(BF16), qk_gain)

        qkv3 = qkv.reshape(b, s, IN_COLS)
        rel = _toeplitz((a_rel_bias[i].astype(F32) * np.float32(LOG2_E))[:, rel_idx], Q_TILE, win_a)
        bias_a = jnp.where(visible_a[:, None], rel[None], NEG_INF)
        ya = _band_attention(qkv3, bias_a, None, q_col=0, k_col=1, v_col=2,
                             n_heads=A_HEADS, n_kv=A_HEADS, n_prev=A_PREV_CHUNKS, name="mixer_a")
        kb_col = (3 * A_WIDTH + B_Q_WIDTH) // B_KV_WIDTH
        sinks = (b_sinks[i].astype(F32) * np.float32(LOG2_E))[b_lane_heads]
        yb = _band_attention(qkv3, bias_b, sinks, q_col=3, k_col=kb_col, v_col=kb_col + 1,
                             n_heads=B_Q_HEADS, n_kv=B_KV_HEADS, n_prev=B_PREV_CHUNKS, name="mixer_b")

        h = _dense_call(
            _merge_ffn2_ple_kernel, n_rows,
            in_specs=[_row_tiled(D_MODEL), _row_tiled(A_WIDTH), _row_tiled(B_Q_WIDTH), _row_tiled(PLE_DIM),
                      _resident((1, D_MODEL)), _resident((D_MODEL, 2 * D_MODEL)), _resident((A_WIDTH, D_MODEL)),
                      _resident((B_Q_WIDTH, D_MODEL)), _resident((D_MODEL, D_MODEL)),
                      _resident((1, D_MODEL)), in_hbm, in_hbm,
                      _resident((1, D_MODEL)), _resident((D_MODEL, D_MODEL)), _resident((PLE_DIM, D_MODEL))],
            out_specs=_row_tiled(D_MODEL), out_shape=row_f32,
            scratch_shapes=[act_scratch] + _ffn_weight_scratch(), name="merge_ffn2_ple",
        )(h1, ya.reshape(n_rows, A_WIDTH), yb.reshape(n_rows, B_Q_WIDTH), p[i].reshape(n_rows, PLE_DIM),
          vec(mix_norm), w_gate[i].astype(BF16), w_proj_a[i].astype(BF16), w_proj_b_i.astype(BF16),
          w_out[i].astype(BF16), vec(ffn2_norm), ffn2_w_gu[i].astype(F32), ffn2_w_down[i].astype(F32),
          vec(ple_norm), w_ple_gate[i].astype(BF16), w_ple_proj[i].astype(BF16))
    return h.reshape(b, s, D_MODEL)
```

```python
import functools

import numpy as np
import jax
import jax.numpy as jnp
from jax import lax
from jax.experimental import pallas as pl
from jax.experimental.pallas import tpu as pltpu

D_MODEL = 1024
D_FF = 2816
PLE_DIM = 256
CHUNK = 64
HEAD_DIM = 64
A_HEADS = 8
A_PREV_CHUNKS = 8
A_MAX_REL = 128
B_Q_HEADS = 8
B_KV_HEADS = 2
B_GROUP = B_Q_HEADS // B_KV_HEADS
B_PREV_CHUNKS = 2
A_WIDTH = A_HEADS * HEAD_DIM
B_Q_WIDTH = B_Q_HEADS * HEAD_DIM
B_KV_WIDTH = B_KV_HEADS * HEAD_DIM
IN_COLS = 3 * A_WIDTH + B_Q_WIDTH + 2 * B_KV_WIDTH
EPS = 1e-6
NEG_INF = -1e30
LOG2_E = 1.4426950408889634

V7X_MXU_DIM = 256
V7X_LANES = 128
V7X_VMEM_BYTES = 64 * 1024 * 1024
VMEM_LIMIT_BYTES = V7X_VMEM_BYTES - 8 * 1024 * 1024

ROW_TILE = 512
FF_CHUNK = V7X_MXU_DIM
UP_STAGE_ROWS = 128
DOWN_STAGE_ROWS = 256
Q_TILE = 2 * CHUNK
Q_STEP = 8 * Q_TILE
SCORE_LOOKAHEAD = 1

BF16 = jnp.bfloat16
F32 = jnp.float32


def _dot(a, b):
    return jnp.dot(a, b, preferred_element_type=F32)


def _sigmoid(x):
    return 1.0 / (1.0 + jnp.exp(-x))


def _rms_norm(h, gain):
    ms = jnp.mean(h * h, axis=-1, keepdims=True)
    return (h * lax.rsqrt(ms + EPS)) * gain


def _swiglu(xn, wgu_ref, wd_ref, act_ref):
    for j in range(D_FF // FF_CHUNK):
        lo = j * FF_CHUNK
        g = _dot(xn, wgu_ref[:, lo:lo + FF_CHUNK])
        u = _dot(xn, wgu_ref[:, D_FF + lo:D_FF + lo + FF_CHUNK])
        act_ref[:, lo:lo + FF_CHUNK] = ((g * _sigmoid(g)) * u).astype(BF16)
    return _dot(act_ref[...], wd_ref[...])


def _fetch_ffn_weights(wgu_hbm, wd_hbm, wgu_ref, wd_ref, stage_gu_ref, stage_d_ref, sem_ref):
    @pl.when(pl.program_id(0) == 0)
    def _():
        for w, (src, dst, stage) in enumerate(((wgu_hbm, wgu_ref, stage_gu_ref),
                                               (wd_hbm, wd_ref, stage_d_ref))):
            rows = stage.shape[1]
            n_chunks = src.shape[0] // rows

            def copy(c, src=src, stage=stage, rows=rows, w=w):
                return pltpu.make_async_copy(src.at[pl.ds(c * rows, rows)], stage.at[c % 2],
                                             sem_ref.at[w, c % 2])

            copy(0).start()
            for c in range(n_chunks):
                if c + 1 < n_chunks:
                    copy(c + 1).start()
                copy(c).wait()
                dst[c * rows:(c + 1) * rows, :] = stage[c % 2].astype(BF16)


def _ffn_weight_scratch():
    return [pltpu.VMEM((D_MODEL, 2 * D_FF), BF16), pltpu.VMEM((D_FF, D_MODEL), BF16),
            pltpu.VMEM((2, UP_STAGE_ROWS, 2 * D_FF), F32), pltpu.VMEM((2, DOWN_STAGE_ROWS, D_MODEL), F32),
            pltpu.SemaphoreType.DMA((2, 2))]


def _head_pair_norm(x, gain):
    first_head = lax.broadcasted_iota(jnp.int32, x.shape, 1) < HEAD_DIM
    sq = x * x
    s0 = jnp.sum(jnp.where(first_head, sq, 0.0), axis=-1, keepdims=True)
    s1 = jnp.sum(jnp.where(first_head, 0.0, sq), axis=-1, keepdims=True)
    ms = jnp.where(first_head, s0, s1) * (1.0 / HEAD_DIM)
    return (x * lax.rsqrt(ms + EPS)) * gain


_QKA_COLS = (0, 2 * A_WIDTH)
_VA_COLS = (2 * A_WIDTH, 3 * A_WIDTH)
_QB_COLS = (3 * A_WIDTH, 3 * A_WIDTH + B_Q_WIDTH)
_KB_COLS = (_QB_COLS[1], _QB_COLS[1] + B_KV_WIDTH)
_KVB_COLS = (_QB_COLS[1], IN_COLS)


def _ffn1_proj_kernel(x_ref, g1_ref, wgu_hbm, wd_hbm, gm_ref, win_ref, qkg_ref,
                      h1_ref, qkv_ref, act_ref, y_ref, wgu_ref, wd_ref, *fetch_scratch):
    _fetch_ffn_weights(wgu_hbm, wd_hbm, wgu_ref, wd_ref, *fetch_scratch)
    x = x_ref[...]
    h1 = x + 0.5 * _swiglu(_rms_norm(x, g1_ref[...]).astype(BF16), wgu_ref, wd_ref, act_ref)
    h1_ref[...] = h1
    u = _rms_norm(h1, gm_ref[...]).astype(BF16)
    for lo, hi in (_QKA_COLS, _QB_COLS):
        y_ref[:, lo:hi] = _dot(u, win_ref[:, lo:hi])
    half = ROW_TILE // 2
    lo, hi = _KVB_COLS
    y_ref[:half, lo:hi] = _dot(u[:half], win_ref[:, lo:hi])
    y_ref[half:, lo:hi] = _dot(u[half:], win_ref[:, lo:hi])
    lo, hi = _VA_COLS
    y_ref[:, lo:hi] = _dot(u, win_ref[:, lo:hi])

    for first, last in (_QKA_COLS, _QB_COLS, _KB_COLS):
        for lo in range(first, last, V7X_LANES):
            cols = slice(lo, lo + V7X_LANES)
            qkv_ref[:, cols] = _head_pair_norm(y_ref[:, cols], qkg_ref[:, cols]).astype(BF16)
    for lo, hi in (_VA_COLS, (_KB_COLS[1], IN_COLS)):
        qkv_ref[:, lo:hi] = y_ref[:, lo:hi].astype(BF16)


def _merge_ffn2_ple_kernel(h1_ref, ya_ref, yb_ref, p_ref, gm_ref, wg_ref, wa_ref, wb_ref, wo_ref,
                           g2_ref, wgu_hbm, wd_hbm, gp_ref, wpg_ref, wpe_ref, out_ref, act_ref,
                           wgu_ref, wd_ref, *fetch_scratch):
    _fetch_ffn_weights(wgu_hbm, wd_hbm, wgu_ref, wd_ref, *fetch_scratch)
    h1 = h1_ref[...]
    u = _rms_norm(h1, gm_ref[...]).astype(BF16)
    ga = _sigmoid(_dot(u, wg_ref[:, :D_MODEL]))
    gb = _sigmoid(_dot(u, wg_ref[:, D_MODEL:]))
    merged = ga * _dot(ya_ref[...], wa_ref[...]) + gb * _dot(yb_ref[...], wb_ref[...])
    h2 = h1 + _dot(merged.astype(BF16), wo_ref[...])
    h3 = h2 + 0.5 * _swiglu(_rms_norm(h2, g2_ref[...]).astype(BF16), wgu_ref, wd_ref, act_ref)
    gate = _sigmoid(_dot(_rms_norm(h3, gp_ref[...]).astype(BF16), wpg_ref[...]))
    out_ref[...] = h3 + gate * _dot(p_ref[...].astype(BF16), wpe_ref[...])


def _band_attention_kernel(*refs, n_pairs, kv_pairs, pad, has_sink):
    if has_sink:
        sink_ref, q_ref, k_ref, v_ref, bias_ref, o_ref, kpad_ref, vext_ref = refs
    else:
        q_ref, k_ref, v_ref, bias_ref, o_ref, kpad_ref, vext_ref = refs
    t = pl.program_id(1)
    window = Q_TILE + pad
    seq = k_ref.shape[1]
    pair = 2 * HEAD_DIM

    @pl.when(t == 0)
    def _():
        kpad_ref[:pad, :] = jnp.zeros((pad, kpad_ref.shape[1]), BF16)
        kpad_ref[pad:, :] = k_ref[0]
        for kp in range(kv_pairs):
            vext_ref[:pad, 2 * pair * kp:2 * pair * kp + pair] = jnp.zeros((pad, pair), BF16)
            vext_ref[pad:, 2 * pair * kp:2 * pair * kp + pair] = v_ref[0, :, pair * kp:pair * (kp + 1)]
            vext_ref[:, 2 * pair * kp + pair:2 * pair * (kp + 1)] = jnp.ones((seq + pad, pair), BF16)

    first_head = lax.broadcasted_iota(jnp.int32, (Q_TILE, pair), 1) < HEAD_DIM
    first_rows = lax.broadcasted_iota(jnp.int32, (2 * Q_TILE, 1), 0) < Q_TILE
    tiles_per_step = Q_STEP // Q_TILE

    def window_start(sub):
        return pl.multiple_of((t * tiles_per_step + sub) * Q_TILE, Q_TILE)

    def scores(sub, pp):
        kp = pp % kv_pairs
        variant = jnp.minimum(t * tiles_per_step + sub, bias_ref.shape[0] - 1)
        q = q_ref[0, sub * Q_TILE:(sub + 1) * Q_TILE, pair * pp:pair * (pp + 1)].astype(F32)
        q2 = jnp.concatenate([jnp.where(first_head, q, 0.0), jnp.where(first_head, 0.0, q)], axis=0)
        k = kpad_ref[pl.ds(window_start(sub), window), pair * kp:pair * (kp + 1)]
        s = lax.dot_general(q2.astype(BF16), k, (((1,), (1,)), ((), ())), preferred_element_type=F32)
        return s + bias_ref[variant, 2 * pp:2 * pp + 2].reshape(2 * Q_TILE, window)

    def finish(sub, pp, s):
        kp = pp % kv_pairs
        v = vext_ref[pl.ds(window_start(sub), window), 2 * pair * kp:2 * pair * (kp + 1)]
        m = jnp.max(s, axis=-1, keepdims=True)
        if has_sink:
            sink = jnp.where(first_rows, sink_ref[2 * pp], sink_ref[2 * pp + 1])
            m = jnp.maximum(m, sink)
        o2 = _dot(jnp.exp2(s - m).astype(BF16), v)
        denom = o2[:, pair:]
        if has_sink:
            denom = denom + jnp.exp2(sink - m)
        o2 = o2[:, :pair] / denom
        o_ref[0, sub * Q_TILE:(sub + 1) * Q_TILE, pair * pp:pair * (pp + 1)] = (
            jnp.where(first_head, o2[:Q_TILE], o2[Q_TILE:]).astype(BF16))

    units = [(sub, pp) for sub in range(tiles_per_step) for pp in range(n_pairs)]
    pending = [scores(*u) for u in units[:SCORE_LOOKAHEAD]]
    for n, u in enumerate(units):
        if n + SCORE_LOOKAHEAD < len(units):
            pending.append(scores(*units[n + SCORE_LOOKAHEAD]))
        finish(*u, pending.pop(0))


def _resident(shape):
    return pl.BlockSpec(shape, lambda *_: (0,) * len(shape), pipeline_mode=pl.Buffered(1))


def _row_tiled(width):
    return pl.BlockSpec((ROW_TILE, width), lambda i: (i, 0))


def _dense_call(kernel, n_rows, in_specs, out_specs, out_shape, scratch_shapes, name):
    return pl.pallas_call(
        kernel,
        grid=(n_rows // ROW_TILE,),
        in_specs=in_specs,
        out_specs=out_specs,
        out_shape=out_shape,
        scratch_shapes=scratch_shapes,
        compiler_params=pltpu.CompilerParams(
            dimension_semantics=("arbitrary",), vmem_limit_bytes=VMEM_LIMIT_BYTES),
        name=name,
    )


def _band_attention(qkv, bias, sinks, *, q_col, k_col, v_col, n_heads, n_kv, n_prev, name):
    b, s, _ = qkv.shape
    pad = n_prev * CHUNK
    qw, kw = n_heads * HEAD_DIM, n_kv * HEAD_DIM
    has_sink = sinks is not None
    in_specs = [
        pl.BlockSpec((1, Q_STEP, qw), lambda bi, ti: (bi, ti, q_col)),
        pl.BlockSpec((1, s, kw), lambda bi, ti: (bi, 0, k_col)),
        pl.BlockSpec((1, s, kw), lambda bi, ti: (bi, 0, v_col)),
        _resident(bias.shape),
    ]
    args = [qkv, qkv, qkv, bias]
    if has_sink:
        in_specs.insert(0, pl.BlockSpec(memory_space=pltpu.SMEM))
        args.insert(0, sinks)
    return pl.pallas_call(
        functools.partial(_band_attention_kernel, n_pairs=n_heads // 2, kv_pairs=n_kv // 2,
                          pad=pad, has_sink=has_sink),
        grid=(b, s // Q_STEP),
        in_specs=in_specs,
        out_specs=pl.BlockSpec((1, Q_STEP, qw), lambda bi, ti: (bi, ti, 0)),
        out_shape=jax.ShapeDtypeStruct((b, s, qw), BF16),
        scratch_shapes=[pltpu.VMEM((s + pad, kw), BF16), pltpu.VMEM((s + pad, 2 * kw), BF16)],
        compiler_params=pltpu.CompilerParams(
            dimension_semantics=("arbitrary", "arbitrary"), vmem_limit_bytes=VMEM_LIMIT_BYTES),
        name=name,
    )(*args)


def _band_geometry(n_prev):
    pad = n_prev * CHUNK
    assert pad % Q_TILE == 0
    i = np.arange(Q_TILE)[:, None]
    j = np.arange(Q_TILE + pad)[None, :]
    dist = i + pad - j
    q_chunk, k_chunk = i // CHUNK, j // CHUNK
    in_band = (k_chunk >= q_chunk) & (k_chunk <= q_chunk + n_prev)
    first_col = pad - Q_TILE * np.arange(pad // Q_TILE + 1)
    visible = in_band[None] & (j[None] >= first_col[:, None, None])
    return dist, visible


def _toeplitz(g, n_rows, n_cols):
    length = n_rows + n_cols - 1
    skew = jnp.tile(g, (1, n_rows + 1))[:, :n_rows * (length + 1)].reshape(-1, n_rows, length + 1)
    return skew[:, :, :n_cols][:, :, ::-1]


def _alibi_slopes(n_heads):
    return np.array([2.0 ** (-8.0 * (h + 1) / n_heads) for h in range(n_heads)], dtype=np.float32)


def kernel(x, p, ffn1_norm, ffn1_w_gu, ffn1_w_down, mix_norm, w_in, a_q_norm, a_k_norm, a_rel_bias, b_q_norm, b_k_norm, b_sinks, w_gate, w_proj_a, w_proj_b, w_out, ffn2_norm, ffn2_w_gu, ffn2_w_down, ple_norm, w_ple_gate, w_ple_proj):
    b, s, d = x.shape
    depth = ffn1_w_gu.shape[0]
    assert d == D_MODEL and s % Q_STEP == 0 and (b * s) % ROW_TILE == 0
    n_rows = b * s
    q_scale = np.float32(HEAD_DIM ** -0.5 * LOG2_E)

    win_a = Q_TILE + A_PREV_CHUNKS * CHUNK
    _, visible_a = _band_geometry(A_PREV_CHUNKS)
    diag = np.arange(Q_TILE + win_a - 1) - (win_a - 1) + A_PREV_CHUNKS * CHUNK
    rel_idx = np.clip(diag, -A_MAX_REL, A_MAX_REL) + A_MAX_REL
    dist_b, visible_b = _band_geometry(B_PREV_CHUNKS)
    b_lane_heads = np.arange(B_Q_HEADS).reshape(B_KV_HEADS, B_GROUP).T.reshape(-1)
    alibi = -_alibi_slopes(B_Q_HEADS)[:, None, None] * np.abs(dist_b).astype(np.float32)[None]
    bias_b = jnp.asarray(np.where(visible_b[:, None], (alibi * np.float32(LOG2_E))[b_lane_heads][None],
                                  np.float32(NEG_INF)))

    row_f32 = jax.ShapeDtypeStruct((n_rows, D_MODEL), F32)
    in_hbm = pl.BlockSpec(memory_space=pl.ANY)
    act_scratch = pltpu.VMEM((ROW_TILE, D_FF), BF16)

    h = x.reshape(n_rows, D_MODEL)
    for i in range(depth):
        vec = lambda g: g[i].astype(F32).reshape(1, -1)
        qk_gain = jnp.concatenate([
            jnp.tile(a_q_norm[i], A_HEADS) * q_scale, jnp.tile(a_k_norm[i], A_HEADS),
            jnp.ones((A_WIDTH,), F32),
            jnp.tile(b_q_norm[i], B_Q_HEADS) * q_scale, jnp.tile(b_k_norm[i], B_KV_HEADS),
            jnp.ones((B_KV_WIDTH,), F32)]).astype(F32).reshape(1, IN_COLS)
        qb_lo = 3 * A_WIDTH
        w_qb = w_in[i][:, qb_lo:qb_lo + B_Q_WIDTH].reshape(D_MODEL, B_KV_HEADS, B_GROUP, HEAD_DIM)
        w_in_i = jnp.concatenate([w_in[i][:, :qb_lo], w_qb.transpose(0, 2, 1, 3).reshape(D_MODEL, B_Q_WIDTH),
                                  w_in[i][:, qb_lo + B_Q_WIDTH:]], axis=1)
        w_proj_b_i = (w_proj_b[i].reshape(B_KV_HEADS, B_GROUP, HEAD_DIM, D_MODEL)
                      .transpose(1, 0, 2, 3).reshape(B_Q_WIDTH, D_MODEL))

        h1, qkv = _dense_call(
            _ffn1_proj_kernel, n_rows,
            in_specs=[_row_tiled(D_MODEL), _resident((1, D_MODEL)), in_hbm, in_hbm,
                      _resident((1, D_MODEL)), _resident((D_MODEL, IN_COLS)), _resident((1, IN_COLS))],
            out_specs=[_row_tiled(D_MODEL), _row_tiled(IN_COLS)],
            out_shape=[row_f32, jax.ShapeDtypeStruct((n_rows, IN_COLS), BF16)],
            scratch_shapes=[act_scratch, pltpu.VMEM((ROW_TILE, IN_COLS), F32)] + _ffn_weight_scratch(),
            name="ffn1_proj",
        )(h, vec(ffn1_norm), ffn1_w_gu[i].astype(F32), ffn1_w_down[i].astype(F32),
          vec(mix_norm), w_in_i.astype(BF16), qk_gain)

        qkv3 = qkv.reshape(b, s, IN_COLS)
        rel = _toeplitz((a_rel_bias[i].astype(F32) * np.float32(LOG2_E))[:, rel_idx], Q_TILE, win_a)
        bias_a = jnp.where(visible_a[:, None], rel[None], NEG_INF)
        ya = _band_attention(qkv3, bias_a, None, q_col=0, k_col=1, v_col=2,
                             n_heads=A_HEADS, n_kv=A_HEADS, n_prev=A_PREV_CHUNKS, name="mixer_a")
        kb_col = (3 * A_WIDTH + B_Q_WIDTH) // B_KV_WIDTH
        sinks = (b_sinks[i].astype(F32) * np.float32(LOG2_E))[b_lane_heads]
        yb = _band_attention(qkv3, bias_b, sinks, q_col=3, k_col=kb_col, v_col=kb_col + 1,
                             n_heads=B_Q_HEADS, n_kv=B_KV_HEADS, n_prev=B_PREV_CHUNKS, name="mixer_b")

        h = _dense_call(
            _merge_ffn2_ple_kernel, n_rows,
            in_specs=[_row_tiled(D_MODEL), _row_tiled(A_WIDTH), _row_tiled(B_Q_WIDTH), _row_tiled(PLE_DIM),
                      _resident((1, D_MODEL)), _resident((D_MODEL, 2 * D_MODEL)), _resident((A_WIDTH, D_MODEL)),
                      _resident((B_Q_WIDTH, D_MODEL)), _resident((D_MODEL, D_MODEL)),
                      _resident((1, D_MODEL)), in_hbm, in_hbm,
                      _resident((1, D_MODEL)), _resident((D_MODEL, D_MODEL)), _resident((PLE_DIM, D_MODEL))],
            out_specs=_row_tiled(D_MODEL), out_shape=row_f32,
            scratch_shapes=[act_scratch] + _ffn_weight_scratch(), name="merge_ffn2_ple",
        )(h1, ya.reshape(n_rows, A_WIDTH), yb.reshape(n_rows, B_Q_WIDTH), p[i].reshape(n_rows, PLE_DIM),
          vec(mix_norm), w_gate[i].astype(BF16), w_proj_a[i].astype(BF16), w_proj_b_i.astype(BF16),
          w_out[i].astype(BF16), vec(ffn2_norm), ffn2_w_gu[i].astype(F32), ffn2_w_down[i].astype(F32),
          vec(ple_norm), w_ple_gate[i].astype(BF16), w_ple_proj[i].astype(BF16))
    return h.reshape(b, s, D_MODEL)
```

```python
import functools

import numpy as np
import jax
import jax.numpy as jnp
from jax import lax
from jax.experimental import pallas as pl
from jax.experimental.pallas import tpu as pltpu

D_MODEL = 1024
D_FF = 2816
PLE_DIM = 256
CHUNK = 64
HEAD_DIM = 64
A_HEADS = 8
A_PREV_CHUNKS = 8
A_MAX_REL = 128
B_Q_HEADS = 8
B_KV_HEADS = 2
B_GROUP = B_Q_HEADS // B_KV_HEADS
B_PREV_CHUNKS = 2
A_WIDTH = A_HEADS * HEAD_DIM
B_Q_WIDTH = B_Q_HEADS * HEAD_DIM
B_KV_WIDTH = B_KV_HEADS * HEAD_DIM
IN_COLS = 3 * A_WIDTH + B_Q_WIDTH + 2 * B_KV_WIDTH
EPS = 1e-6
NEG_INF = -1e30
LOG2_E = 1.4426950408889634

V7X_MXU_DIM = 256
V7X_LANES = 128
V7X_VMEM_BYTES = 64 * 1024 * 1024
VMEM_LIMIT_BYTES = V7X_VMEM_BYTES - 8 * 1024 * 1024

ROW_TILE = 512
FF_CHUNK = V7X_MXU_DIM
UP_STAGE_ROWS = 128
DOWN_STAGE_ROWS = 256
Q_TILE = 2 * CHUNK
Q_STEP = 8 * Q_TILE
SCORE_LOOKAHEAD = 1

BF16 = jnp.bfloat16
F32 = jnp.float32


def _dot(a, b):
    return jnp.dot(a, b, preferred_element_type=F32)


def _sigmoid(x):
    return 1.0 / (1.0 + jnp.exp(-x))


def _rms_norm(h, gain):
    ms = jnp.mean(h * h, axis=-1, keepdims=True)
    return (h * lax.rsqrt(ms + EPS)) * gain


def _swiglu(xn, wgu_ref, wd_ref, act_ref):
    for j in range(D_FF // FF_CHUNK):
        lo = j * FF_CHUNK
        g = _dot(xn, wgu_ref[:, lo:lo + FF_CHUNK])
        u = _dot(xn, wgu_ref[:, D_FF + lo:D_FF + lo + FF_CHUNK])
        act_ref[:, lo:lo + FF_CHUNK] = ((g * _sigmoid(g)) * u).astype(BF16)
    return _dot(act_ref[...], wd_ref[...])


def _fetch_ffn_weights(wgu_hbm, wd_hbm, wgu_ref, wd_ref, stage_gu_ref, stage_d_ref, sem_ref):
    @pl.when(pl.program_id(0) == 0)
    def _():
        for w, (src, dst, stage) in enumerate(((wgu_hbm, wgu_ref, stage_gu_ref),
                                               (wd_hbm, wd_ref, stage_d_ref))):
            rows = stage.shape[1]
            n_chunks = src.shape[0] // rows

            def copy(c, src=src, stage=stage, rows=rows, w=w):
                return pltpu.make_async_copy(src.at[pl.ds(c * rows, rows)], stage.at[c % 2],
                                             sem_ref.at[w, c % 2])

            copy(0).start()
            for c in range(n_chunks):
                if c + 1 < n_chunks:
                    copy(c + 1).start()
                copy(c).wait()
                dst[c * rows:(c + 1) * rows, :] = stage[c % 2].astype(BF16)


def _ffn_weight_scratch():
    return [pltpu.VMEM((D_MODEL, 2 * D_FF), BF16), pltpu.VMEM((D_FF, D_MODEL), BF16),
            pltpu.VMEM((2, UP_STAGE_ROWS, 2 * D_FF), F32), pltpu.VMEM((2, DOWN_STAGE_ROWS, D_MODEL), F32),
            pltpu.SemaphoreType.DMA((2, 2))]


def _head_pair_norm(x, gain):
    first_head = lax.broadcasted_iota(jnp.int32, x.shape, 1) < HEAD_DIM
    sq = x * x
    s0 = jnp.sum(jnp.where(first_head, sq, 0.0), axis=-1, keepdims=True)
    s1 = jnp.sum(jnp.where(first_head, 0.0, sq), axis=-1, keepdims=True)
    ms = jnp.where(first_head, s0, s1) * (1.0 / HEAD_DIM)
    return (x * lax.rsqrt(ms + EPS)) * gain


_QKA_COLS = (0, 2 * A_WIDTH)
_VA_COLS = (2 * A_WIDTH, 3 * A_WIDTH)
_QB_COLS = (3 * A_WIDTH, 3 * A_WIDTH + B_Q_WIDTH)
_KB_COLS = (_QB_COLS[1], _QB_COLS[1] + B_KV_WIDTH)
_KVB_COLS = (_QB_COLS[1], IN_COLS)
KT_ROWS = A_WIDTH + B_KV_WIDTH


def _is_key_col(col):
    return A_WIDTH <= col < 2 * A_WIDTH or _KB_COLS[0] <= col < _KB_COLS[1]


def _ffn1_proj_kernel(x_ref, g1_ref, wgu_hbm, wd_hbm, gm_ref, win_ref, qkg_ref,
                      h1_ref, qkv_ref, kt_ref, act_ref, y_ref, wgu_ref, wd_ref, *fetch_scratch):
    _fetch_ffn_weights(wgu_hbm, wd_hbm, wgu_ref, wd_ref, *fetch_scratch)
    x = x_ref[...]
    h1 = x + 0.5 * _swiglu(_rms_norm(x, g1_ref[...]).astype(BF16), wgu_ref, wd_ref, act_ref)
    h1_ref[...] = h1
    u = _rms_norm(h1, gm_ref[...]).astype(BF16)
    for lo, hi in (_QKA_COLS, _QB_COLS):
        y_ref[:, lo:hi] = _dot(u, win_ref[:, lo:hi])
    half = ROW_TILE // 2
    lo, hi = _KVB_COLS
    y_ref[:half, lo:hi] = _dot(u[:half], win_ref[:, lo:hi])
    y_ref[half:, lo:hi] = _dot(u[half:], win_ref[:, lo:hi])
    lo, hi = _VA_COLS
    y_ref[:, lo:hi] = _dot(u, win_ref[:, lo:hi])

    kt_row = 0
    for first, last in (_QKA_COLS, _QB_COLS, _KB_COLS):
        for lo in range(first, last, V7X_LANES):
            cols = slice(lo, lo + V7X_LANES)
            normed = _head_pair_norm(y_ref[:, cols], qkg_ref[:, cols])
            qkv_ref[:, cols] = normed.astype(BF16)
            if _is_key_col(lo):
                kt_ref[0, kt_row:kt_row + V7X_LANES, :] = normed.T.astype(BF16)
                kt_row += V7X_LANES
    for lo, hi in (_VA_COLS, (_KB_COLS[1], IN_COLS)):
        qkv_ref[:, lo:hi] = y_ref[:, lo:hi].astype(BF16)


def _merge_ffn2_ple_kernel(h1_ref, ya_ref, yb_ref, p_ref, gm_ref, wg_ref, wa_ref, wb_ref, wo_ref,
                           g2_ref, wgu_hbm, wd_hbm, gp_ref, wpg_ref, wpe_ref, out_ref, act_ref,
                           wgu_ref, wd_ref, *fetch_scratch):
    _fetch_ffn_weights(wgu_hbm, wd_hbm, wgu_ref, wd_ref, *fetch_scratch)
    h1 = h1_ref[...]
    u = _rms_norm(h1, gm_ref[...]).astype(BF16)
    ga = _sigmoid(_dot(u, wg_ref[:, :D_MODEL]))
    gb = _sigmoid(_dot(u, wg_ref[:, D_MODEL:]))
    merged = ga * _dot(ya_ref[...], wa_ref[...]) + gb * _dot(yb_ref[...], wb_ref[...])
    h2 = h1 + _dot(merged.astype(BF16), wo_ref[...])
    h3 = h2 + 0.5 * _swiglu(_rms_norm(h2, g2_ref[...]).astype(BF16), wgu_ref, wd_ref, act_ref)
    gate = _sigmoid(_dot(_rms_norm(h3, gp_ref[...]).astype(BF16), wpg_ref[...]))
    out_ref[...] = h3 + gate * _dot(p_ref[...].astype(BF16), wpe_ref[...])


def _band_attention_kernel(*refs, n_pairs, kv_pairs, pad, has_sink):
    if has_sink:
        sink_ref, q_ref, kt_ref, v_ref, bias_ref, o_ref, kpad_ref, vext_ref = refs
    else:
        q_ref, kt_ref, v_ref, bias_ref, o_ref, kpad_ref, vext_ref = refs
    t = pl.program_id(1)
    window = Q_TILE + pad
    seq = v_ref.shape[1]
    pair = 2 * HEAD_DIM

    @pl.when(t == 0)
    def _():
        kpad_ref[:, :pad] = jnp.zeros((kpad_ref.shape[0], pad), BF16)
        kpad_ref[:, pad:] = kt_ref[0]
        for kp in range(kv_pairs):
            vext_ref[:pad, 2 * pair * kp:2 * pair * kp + pair] = jnp.zeros((pad, pair), BF16)
            vext_ref[pad:, 2 * pair * kp:2 * pair * kp + pair] = v_ref[0, :, pair * kp:pair * (kp + 1)]
            vext_ref[:, 2 * pair * kp + pair:2 * pair * (kp + 1)] = jnp.ones((seq + pad, pair), BF16)

    first_head = lax.broadcasted_iota(jnp.int32, (Q_TILE, pair), 1) < HEAD_DIM
    first_rows = lax.broadcasted_iota(jnp.int32, (2 * Q_TILE, 1), 0) < Q_TILE
    tiles_per_step = Q_STEP // Q_TILE

    def window_start(sub):
        return pl.multiple_of((t * tiles_per_step + sub) * Q_TILE, Q_TILE)

    def scores(sub, pp):
        kp = pp % kv_pairs
        variant = jnp.minimum(t * tiles_per_step + sub, bias_ref.shape[0] - 1)
        q = q_ref[0, sub * Q_TILE:(sub + 1) * Q_TILE, pair * pp:pair * (pp + 1)].astype(F32)
        q2 = jnp.concatenate([jnp.where(first_head, q, 0.0), jnp.where(first_head, 0.0, q)], axis=0)
        kt = kpad_ref[pair * kp:pair * (kp + 1), pl.ds(window_start(sub), window)]
        return _dot(q2.astype(BF16), kt) + bias_ref[variant, 2 * pp:2 * pp + 2].reshape(2 * Q_TILE, window)

    def finish(sub, pp, s):
        kp = pp % kv_pairs
        v = vext_ref[pl.ds(window_start(sub), window), 2 * pair * kp:2 * pair * (kp + 1)]
        m = jnp.max(s, axis=-1, keepdims=True)
        if has_sink:
            sink = jnp.where(first_rows, sink_ref[2 * pp], sink_ref[2 * pp + 1])
            m = jnp.maximum(m, sink)
        o2 = _dot(jnp.exp2(s - m).astype(BF16), v)
        denom = o2[:, pair:]
        if has_sink:
            denom = denom + jnp.exp2(sink - m)
        o2 = o2[:, :pair] / denom
        o_ref[0, sub * Q_TILE:(sub + 1) * Q_TILE, pair * pp:pair * (pp + 1)] = (
            jnp.where(first_head, o2[:Q_TILE], o2[Q_TILE:]).astype(BF16))

    units = [(sub, pp) for sub in range(tiles_per_step) for pp in range(n_pairs)]
    pending = [scores(*u) for u in units[:SCORE_LOOKAHEAD]]
    for n, u in enumerate(units):
        if n + SCORE_LOOKAHEAD < len(units):
            pending.append(scores(*units[n + SCORE_LOOKAHEAD]))
        finish(*u, pending.pop(0))


def _resident(shape):
    return pl.BlockSpec(shape, lambda *_: (0,) * len(shape), pipeline_mode=pl.Buffered(1))


def _row_tiled(width):
    return pl.BlockSpec((ROW_TILE, width), lambda i: (i, 0))


def _dense_call(kernel, n_rows, in_specs, out_specs, out_shape, scratch_shapes, name):
    return pl.pallas_call(
        kernel,
        grid=(n_rows // ROW_TILE,),
        in_specs=in_specs,
        out_specs=out_specs,
        out_shape=out_shape,
        scratch_shapes=scratch_shapes,
        compiler_params=pltpu.CompilerParams(
            dimension_semantics=("arbitrary",), vmem_limit_bytes=VMEM_LIMIT_BYTES),
        name=name,
    )


def _band_attention(qkv, kt, bias, sinks, *, q_col, kt_row, v_col, n_heads, n_kv, n_prev, name):
    b, s, _ = qkv.shape
    pad = n_prev * CHUNK
    qw, kw = n_heads * HEAD_DIM, n_kv * HEAD_DIM
    has_sink = sinks is not None
    in_specs = [
        pl.BlockSpec((1, Q_STEP, qw), lambda bi, ti: (bi, ti, q_col)),
        pl.BlockSpec((1, kw, s), lambda bi, ti: (bi, kt_row, 0)),
        pl.BlockSpec((1, s, kw), lambda bi, ti: (bi, 0, v_col)),
        _resident(bias.shape),
    ]
    args = [qkv, kt, qkv, bias]
    if has_sink:
        in_specs.insert(0, pl.BlockSpec(memory_space=pltpu.SMEM))
        args.insert(0, sinks)
    return pl.pallas_call(
        functools.partial(_band_attention_kernel, n_pairs=n_heads // 2, kv_pairs=n_kv // 2,
                          pad=pad, has_sink=has_sink),
        grid=(b, s // Q_STEP),
        in_specs=in_specs,
        out_specs=pl.BlockSpec((1, Q_STEP, qw), lambda bi, ti: (bi, ti, 0)),
        out_shape=jax.ShapeDtypeStruct((b, s, qw), BF16),
        scratch_shapes=[pltpu.VMEM((kw, s + pad), BF16), pltpu.VMEM((s + pad, 2 * kw), BF16)],
        compiler_params=pltpu.CompilerParams(
            dimension_semantics=("arbitrary", "arbitrary"), vmem_limit_bytes=VMEM_LIMIT_BYTES),
        name=name,
    )(*args)


def _band_geometry(n_prev):
    pad = n_prev * CHUNK
    assert pad % Q_TILE == 0
    i = np.arange(Q_TILE)[:, None]
    j = np.arange(Q_TILE + pad)[None, :]
    dist = i + pad - j
    q_chunk, k_chunk = i // CHUNK, j // CHUNK
    in_band = (k_chunk >= q_chunk) & (k_chunk <= q_chunk + n_prev)
    first_col = pad - Q_TILE * np.arange(pad // Q_TILE + 1)
    visible = in_band[None] & (j[None] >= first_col[:, None, None])
    return dist, visible


def _toeplitz(g, n_rows, n_cols):
    length = n_rows + n_cols - 1
    skew = jnp.tile(g, (1, n_rows + 1))[:, :n_rows * (length + 1)].reshape(-1, n_rows, length + 1)
    return skew[:, :, :n_cols][:, :, ::-1]


def _alibi_slopes(n_heads):
    return np.array([2.0 ** (-8.0 * (h + 1) / n_heads) for h in range(n_heads)], dtype=np.float32)


def kernel(x, p, ffn1_norm, ffn1_w_gu, ffn1_w_down, mix_norm, w_in, a_q_norm, a_k_norm, a_rel_bias, b_q_norm, b_k_norm, b_sinks, w_gate, w_proj_a, w_proj_b, w_out, ffn2_norm, ffn2_w_gu, ffn2_w_down, ple_norm, w_ple_gate, w_ple_proj):
    b, s, d = x.shape
    depth = ffn1_w_gu.shape[0]
    assert d == D_MODEL and s % Q_STEP == 0 and s % ROW_TILE == 0
    n_rows = b * s
    q_scale = np.float32(HEAD_DIM ** -0.5 * LOG2_E)

    win_a = Q_TILE + A_PREV_CHUNKS * CHUNK
    _, visible_a = _band_geometry(A_PREV_CHUNKS)
    diag = np.arange(Q_TILE + win_a - 1) - (win_a - 1) + A_PREV_CHUNKS * CHUNK
    rel_idx = np.clip(diag, -A_MAX_REL, A_MAX_REL) + A_MAX_REL
    dist_b, visible_b = _band_geometry(B_PREV_CHUNKS)
    b_lane_heads = np.arange(B_Q_HEADS).reshape(B_KV_HEADS, B_GROUP).T.reshape(-1)
    alibi = -_alibi_slopes(B_Q_HEADS)[:, None, None] * np.abs(dist_b).astype(np.float32)[None]
    bias_b = jnp.asarray(np.where(visible_b[:, None], (alibi * np.float32(LOG2_E))[b_lane_heads][None],
                                  np.float32(NEG_INF)))

    row_f32 = jax.ShapeDtypeStruct((n_rows, D_MODEL), F32)
    in_hbm = pl.BlockSpec(memory_space=pl.ANY)
    act_scratch = pltpu.VMEM((ROW_TILE, D_FF), BF16)

    h = x.reshape(n_rows, D_MODEL)
    for i in range(depth):
        vec = lambda g: g[i].astype(F32).reshape(1, -1)
        qk_gain = jnp.concatenate([
            jnp.tile(a_q_norm[i], A_HEADS) * q_scale, jnp.tile(a_k_norm[i], A_HEADS),
            jnp.ones((A_WIDTH,), F32),
            jnp.tile(b_q_norm[i], B_Q_HEADS) * q_scale, jnp.tile(b_k_norm[i], B_KV_HEADS),
            jnp.ones((B_KV_WIDTH,), F32)]).astype(F32).reshape(1, IN_COLS)
        qb_lo = 3 * A_WIDTH
        w_qb = w_in[i][:, qb_lo:qb_lo + B_Q_WIDTH].reshape(D_MODEL, B_KV_HEADS, B_GROUP, HEAD_DIM)
        w_in_i = jnp.concatenate([w_in[i][:, :qb_lo], w_qb.transpose(0, 2, 1, 3).reshape(D_MODEL, B_Q_WIDTH),
                                  w_in[i][:, qb_lo + B_Q_WIDTH:]], axis=1)
        w_proj_b_i = (w_proj_b[i].reshape(B_KV_HEADS, B_GROUP, HEAD_DIM, D_MODEL)
                      .transpose(1, 0, 2, 3).reshape(B_Q_WIDTH, D_MODEL))

        tiles_per_seq = s // ROW_TILE
        h1, qkv, kt = _dense_call(
            _ffn1_proj_kernel, n_rows,
            in_specs=[_row_tiled(D_MODEL), _resident((1, D_MODEL)), in_hbm, in_hbm,
                      _resident((1, D_MODEL)), _resident((D_MODEL, IN_COLS)), _resident((1, IN_COLS))],
            out_specs=[_row_tiled(D_MODEL), _row_tiled(IN_COLS),
                       pl.BlockSpec((1, KT_ROWS, ROW_TILE),
                                    lambda r: (r // tiles_per_seq, 0, r % tiles_per_seq))],
            out_shape=[row_f32, jax.ShapeDtypeStruct((n_rows, IN_COLS), BF16),
                       jax.ShapeDtypeStruct((b, KT_ROWS, s), BF16)],
            scratch_shapes=[act_scratch, pltpu.VMEM((ROW_TILE, IN_COLS), F32)] + _ffn_weight_scratch(),
            name="ffn1_proj",
        )(h, vec(ffn1_norm), ffn1_w_gu[i].astype(F32), ffn1_w_down[i].astype(F32),
          vec(mix_norm), w_in_i.astype(BF16), qk_gain)

        qkv3 = qkv.reshape(b, s, IN_COLS)
        rel = _toeplitz((a_rel_bias[i].astype(F32) * np.float32(LOG2_E))[:, rel_idx], Q_TILE, win_a)
        bias_a = jnp.where(visible_a[:, None], rel[None], NEG_INF)
        ya = _band_attention(qkv3, kt, bias_a, None, q_col=0, kt_row=0, v_col=2,
                             n_heads=A_HEADS, n_kv=A_HEADS, n_prev=A_PREV_CHUNKS, name="mixer_a")
        sinks = (b_sinks[i].astype(F32) * np.float32(LOG2_E))[b_lane_heads]
        yb = _band_attention(qkv3, kt, bias_b, sinks, q_col=3, kt_row=A_WIDTH // B_KV_WIDTH,
                             v_col=_KVB_COLS[0] // B_KV_WIDTH + 1,
                             n_heads=B_Q_HEADS, n_kv=B_KV_HEADS, n_prev=B_PREV_CHUNKS, name="mixer_b")

        h = _dense_call(
            _merge_ffn2_ple_kernel, n_rows,
            in_specs=[_row_tiled(D_MODEL), _row_tiled(A_WIDTH), _row_tiled(B_Q_WIDTH), _row_tiled(PLE_DIM),
                      _resident((1, D_MODEL)), _resident((D_MODEL, 2 * D_MODEL)), _resident((A_WIDTH, D_MODEL)),
                      _resident((B_Q_WIDTH, D_MODEL)), _resident((D_MODEL, D_MODEL)),
                      _resident((1, D_MODEL)), in_hbm, in_hbm,
                      _resident((1, D_MODEL)), _resident((D_MODEL, D_MODEL)), _resident((PLE_DIM, D_MODEL))],
            out_specs=_row_tiled(D_MODEL), out_shape=row_f32,
            scratch_shapes=[act_scratch] + _ffn_weight_scratch(), name="merge_ffn2_ple",
        )(h1, ya.reshape(n_rows, A_WIDTH), yb.reshape(n_rows, B_Q_WIDTH), p[i].reshape(n_rows, PLE_DIM),
          vec(mix_norm), w_gate[i].astype(BF16), w_proj_a[i].astype(BF16), w_proj_b_i.astype(BF16),
          w_out[i].astype(BF16), vec(ffn2_norm), ffn2_w_gu[i].astype(F32), ffn2_w_down[i].astype(F32),
          vec(ple_norm), w_ple_gate[i].astype(BF16), w_ple_proj[i].astype(BF16))
    return h.reshape(b, s, D_MODEL)
```

```python
import functools

import numpy as np
import jax
import jax.numpy as jnp
from jax import lax
from jax.experimental import pallas as pl
from jax.experimental.pallas import tpu as pltpu

D_MODEL = 1024
D_FF = 2816
PLE_DIM = 256
CHUNK = 64
HEAD_DIM = 64
A_HEADS = 8
A_PREV_CHUNKS = 8
A_MAX_REL = 128
B_Q_HEADS = 8
B_KV_HEADS = 2
B_GROUP = B_Q_HEADS // B_KV_HEADS
B_PREV_CHUNKS = 2
A_WIDTH = A_HEADS * HEAD_DIM
B_Q_WIDTH = B_Q_HEADS * HEAD_DIM
B_KV_WIDTH = B_KV_HEADS * HEAD_DIM
IN_COLS = 3 * A_WIDTH + B_Q_WIDTH + 2 * B_KV_WIDTH
EPS = 1e-6
NEG_INF = -1e30
LOG2_E = 1.4426950408889634

V7X_MXU_DIM = 256
V7X_LANES = 128
V7X_VMEM_BYTES = 64 * 1024 * 1024
VMEM_LIMIT_BYTES = V7X_VMEM_BYTES - 8 * 1024 * 1024

ROW_TILE = 512
FF_CHUNK = V7X_MXU_DIM
UP_STAGE_ROWS = 128
DOWN_STAGE_ROWS = 256
Q_TILE = 2 * CHUNK
SCORE_LOOKAHEAD = 1

BF16 = jnp.bfloat16
F32 = jnp.float32


def _dot(a, b):
    return jnp.dot(a, b, preferred_element_type=F32)


def _sigmoid(x):
    return 1.0 / (1.0 + jnp.exp(-x))


def _rms_norm(h, gain):
    ms = jnp.mean(h * h, axis=-1, keepdims=True)
    return (h * lax.rsqrt(ms + EPS)) * gain


def _swiglu(xn, wgu_ref, wd_ref, act_ref):
    for j in range(D_FF // FF_CHUNK):
        lo = j * FF_CHUNK
        g = _dot(xn, wgu_ref[:, lo:lo + FF_CHUNK])
        u = _dot(xn, wgu_ref[:, D_FF + lo:D_FF + lo + FF_CHUNK])
        act_ref[:, lo:lo + FF_CHUNK] = ((g * _sigmoid(g)) * u).astype(BF16)
    return _dot(act_ref[...], wd_ref[...])


def _fetch_ffn_weights(wgu_hbm, wd_hbm, wgu_ref, wd_ref, stage_gu_ref, stage_d_ref, sem_ref):
    @pl.when(pl.program_id(0) == 0)
    def _():
        for w, (src, dst, stage) in enumerate(((wgu_hbm, wgu_ref, stage_gu_ref),
                                               (wd_hbm, wd_ref, stage_d_ref))):
            rows = stage.shape[1]
            n_chunks = src.shape[0] // rows

            def copy(c, src=src, stage=stage, rows=rows, w=w):
                return pltpu.make_async_copy(src.at[pl.ds(c * rows, rows)], stage.at[c % 2],
                                             sem_ref.at[w, c % 2])

            copy(0).start()
            for c in range(n_chunks):
                if c + 1 < n_chunks:
                    copy(c + 1).start()
                copy(c).wait()
                dst[c * rows:(c + 1) * rows, :] = stage[c % 2].astype(BF16)


def _ffn_weight_scratch():
    return [pltpu.VMEM((D_MODEL, 2 * D_FF), BF16), pltpu.VMEM((D_FF, D_MODEL), BF16),
            pltpu.VMEM((2, UP_STAGE_ROWS, 2 * D_FF), F32), pltpu.VMEM((2, DOWN_STAGE_ROWS, D_MODEL), F32),
            pltpu.SemaphoreType.DMA((2, 2))]


def _head_pair_norm(x, gain):
    first_head = lax.broadcasted_iota(jnp.int32, x.shape, 1) < HEAD_DIM
    sq = x * x
    s0 = jnp.sum(jnp.where(first_head, sq, 0.0), axis=-1, keepdims=True)
    s1 = jnp.sum(jnp.where(first_head, 0.0, sq), axis=-1, keepdims=True)
    ms = jnp.where(first_head, s0, s1) * (1.0 / HEAD_DIM)
    return (x * lax.rsqrt(ms + EPS)) * gain


_QKA_COLS = (0, 2 * A_WIDTH)
_VA_COLS = (2 * A_WIDTH, 3 * A_WIDTH)
_QB_COLS = (3 * A_WIDTH, 3 * A_WIDTH + B_Q_WIDTH)
_KB_COLS = (_QB_COLS[1], _QB_COLS[1] + B_KV_WIDTH)
_KVB_COLS = (_QB_COLS[1], IN_COLS)


def _ffn1_proj_kernel(x_ref, g1_ref, wgu_hbm, wd_hbm, gm_ref, win_ref, qkg_ref,
                      h1_ref, qkv_ref, act_ref, y_ref, wgu_ref, wd_ref, *fetch_scratch):
    _fetch_ffn_weights(wgu_hbm, wd_hbm, wgu_ref, wd_ref, *fetch_scratch)
    x = x_ref[...]
    h1 = x + 0.5 * _swiglu(_rms_norm(x, g1_ref[...]).astype(BF16), wgu_ref, wd_ref, act_ref)
    h1_ref[...] = h1
    u = _rms_norm(h1, gm_ref[...]).astype(BF16)
    for lo, hi in (_QKA_COLS, _QB_COLS):
        y_ref[:, lo:hi] = _dot(u, win_ref[:, lo:hi])
    half = ROW_TILE // 2
    lo, hi = _KVB_COLS
    y_ref[:half, lo:hi] = _dot(u[:half], win_ref[:, lo:hi])
    y_ref[half:, lo:hi] = _dot(u[half:], win_ref[:, lo:hi])
    lo, hi = _VA_COLS
    y_ref[:, lo:hi] = _dot(u, win_ref[:, lo:hi])

    for first, last in (_QKA_COLS, _QB_COLS, _KB_COLS):
        for lo in range(first, last, V7X_LANES):
            cols = slice(lo, lo + V7X_LANES)
            qkv_ref[:, cols] = _head_pair_norm(y_ref[:, cols], qkg_ref[:, cols]).astype(BF16)
    for lo, hi in (_VA_COLS, (_KB_COLS[1], IN_COLS)):
        qkv_ref[:, lo:hi] = y_ref[:, lo:hi].astype(BF16)


def _merge_ffn2_ple_kernel(h1_ref, ya_ref, yb_ref, p_ref, gm_ref, wg_ref, wa_ref, wb_ref, wo_ref,
                           g2_ref, wgu_hbm, wd_hbm, gp_ref, wpg_ref, wpe_ref, out_ref, act_ref,
                           wgu_ref, wd_ref, *fetch_scratch):
    _fetch_ffn_weights(wgu_hbm, wd_hbm, wgu_ref, wd_ref, *fetch_scratch)
    h1 = h1_ref[...]
    u = _rms_norm(h1, gm_ref[...]).astype(BF16)
    ga = _sigmoid(_dot(u, wg_ref[:, :D_MODEL]))
    gb = _sigmoid(_dot(u, wg_ref[:, D_MODEL:]))
    merged = ga * _dot(ya_ref[...], wa_ref[...]) + gb * _dot(yb_ref[...], wb_ref[...])
    h2 = h1 + _dot(merged.astype(BF16), wo_ref[...])
    h3 = h2 + 0.5 * _swiglu(_rms_norm(h2, g2_ref[...]).astype(BF16), wgu_ref, wd_ref, act_ref)
    gate = _sigmoid(_dot(_rms_norm(h3, gp_ref[...]).astype(BF16), wpg_ref[...]))
    out_ref[...] = h3 + gate * _dot(p_ref[...].astype(BF16), wpe_ref[...])


def _band_attention_kernel(*refs, n_pairs, kv_pairs, pad, has_sink):
    if has_sink:
        sink_ref, q_ref, k_ref, v_ref, bias_ref, o_ref = refs
    else:
        q_ref, k_ref, v_ref, bias_ref, o_ref = refs
    seq = k_ref.shape[1]
    pair = 2 * HEAD_DIM
    first_head = lax.broadcasted_iota(jnp.int32, (Q_TILE, pair), 1) < HEAD_DIM
    first_rows = lax.broadcasted_iota(jnp.int32, (2 * Q_TILE, 1), 0) < Q_TILE
    ones = jnp.ones((Q_TILE + pad, pair), BF16)

    def key_rows(tile):
        return max(tile * Q_TILE - pad, 0), (tile + 1) * Q_TILE

    def scores(tile, pp):
        kp = pp % kv_pairs
        lo, hi = key_rows(tile)
        q = q_ref[0, tile * Q_TILE:(tile + 1) * Q_TILE, pair * pp:pair * (pp + 1)].astype(F32)
        q2 = jnp.concatenate([jnp.where(first_head, q, 0.0), jnp.where(first_head, 0.0, q)], axis=0)
        k = k_ref[0, lo:hi, pair * kp:pair * (kp + 1)]
        s = lax.dot_general(q2.astype(BF16), k, (((1,), (1,)), ((), ())), preferred_element_type=F32)
        clipped = Q_TILE + pad - (hi - lo)
        return s + bias_ref[2 * pp:2 * pp + 2, :, clipped:].reshape(2 * Q_TILE, hi - lo)

    def finish(tile, pp, s):
        kp = pp % kv_pairs
        lo, hi = key_rows(tile)
        v = jnp.concatenate([v_ref[0, lo:hi, pair * kp:pair * (kp + 1)], ones[:hi - lo]], axis=1)
        m = jnp.max(s, axis=-1, keepdims=True)
        if has_sink:
            sink = jnp.where(first_rows, sink_ref[2 * pp], sink_ref[2 * pp + 1])
            m = jnp.maximum(m, sink)
        o2 = _dot(jnp.exp2(s - m).astype(BF16), v)
        denom = o2[:, pair:]
        if has_sink:
            denom = denom + jnp.exp2(sink - m)
        o2 = o2[:, :pair] / denom
        o_ref[0, tile * Q_TILE:(tile + 1) * Q_TILE, pair * pp:pair * (pp + 1)] = (
            jnp.where(first_head, o2[:Q_TILE], o2[Q_TILE:]).astype(BF16))

    units = [(tile, pp) for tile in range(seq // Q_TILE) for pp in range(n_pairs)]
    pending = [scores(*u) for u in units[:SCORE_LOOKAHEAD]]
    for n, u in enumerate(units):
        if n + SCORE_LOOKAHEAD < len(units):
            pending.append(scores(*units[n + SCORE_LOOKAHEAD]))
        finish(*u, pending.pop(0))


def _resident(shape):
    return pl.BlockSpec(shape, lambda *_: (0,) * len(shape), pipeline_mode=pl.Buffered(1))


def _row_tiled(width):
    return pl.BlockSpec((ROW_TILE, width), lambda i: (i, 0))


def _dense_call(kernel, n_rows, in_specs, out_specs, out_shape, scratch_shapes, name):
    return pl.pallas_call(
        kernel,
        grid=(n_rows // ROW_TILE,),
        in_specs=in_specs,
        out_specs=out_specs,
        out_shape=out_shape,
        scratch_shapes=scratch_shapes,
        compiler_params=pltpu.CompilerParams(
            dimension_semantics=("arbitrary",), vmem_limit_bytes=VMEM_LIMIT_BYTES),
        name=name,
    )


def _band_attention(qkv, bias, sinks, *, q_col, k_col, v_col, n_heads, n_kv, n_prev, name):
    b, s, _ = qkv.shape
    pad = n_prev * CHUNK
    qw, kw = n_heads * HEAD_DIM, n_kv * HEAD_DIM
    has_sink = sinks is not None
    in_specs = [
        pl.BlockSpec((1, s, qw), lambda bi: (bi, 0, q_col)),
        pl.BlockSpec((1, s, kw), lambda bi: (bi, 0, k_col)),
        pl.BlockSpec((1, s, kw), lambda bi: (bi, 0, v_col)),
        _resident(bias.shape),
    ]
    args = [qkv, qkv, qkv, bias]
    if has_sink:
        in_specs.insert(0, pl.BlockSpec(memory_space=pltpu.SMEM))
        args.insert(0, sinks)
    return pl.pallas_call(
        functools.partial(_band_attention_kernel, n_pairs=n_heads // 2, kv_pairs=n_kv // 2,
                          pad=pad, has_sink=has_sink),
        grid=(b,),
        in_specs=in_specs,
        out_specs=pl.BlockSpec((1, s, qw), lambda bi: (bi, 0, 0)),
        out_shape=jax.ShapeDtypeStruct((b, s, qw), BF16),
        compiler_params=pltpu.CompilerParams(
            dimension_semantics=("arbitrary",), vmem_limit_bytes=VMEM_LIMIT_BYTES),
        name=name,
    )(*args)


def _band_geometry(n_prev):
    pad = n_prev * CHUNK
    assert pad % Q_TILE == 0
    i = np.arange(Q_TILE)[:, None]
    j = np.arange(Q_TILE + pad)[None, :]
    dist = i + pad - j
    q_chunk, k_chunk = i // CHUNK, j // CHUNK
    in_band = (k_chunk >= q_chunk) & (k_chunk <= q_chunk + n_prev)
    return dist, in_band


def _toeplitz(g, n_rows, n_cols):
    length = n_rows + n_cols - 1
    skew = jnp.tile(g, (1, n_rows + 1))[:, :n_rows * (length + 1)].reshape(-1, n_rows, length + 1)
    return skew[:, :, :n_cols][:, :, ::-1]


def _alibi_slopes(n_heads):
    return np.array([2.0 ** (-8.0 * (h + 1) / n_heads) for h in range(n_heads)], dtype=np.float32)


def kernel(x, p, ffn1_norm, ffn1_w_gu, ffn1_w_down, mix_norm, w_in, a_q_norm, a_k_norm, a_rel_bias, b_q_norm, b_k_norm, b_sinks, w_gate, w_proj_a, w_proj_b, w_out, ffn2_norm, ffn2_w_gu, ffn2_w_down, ple_norm, w_ple_gate, w_ple_proj):
    b, s, d = x.shape
    depth = ffn1_w_gu.shape[0]
    assert d == D_MODEL and s % Q_TILE == 0 and (b * s) % ROW_TILE == 0
    n_rows = b * s
    q_scale = np.float32(HEAD_DIM ** -0.5 * LOG2_E)

    win_a = Q_TILE + A_PREV_CHUNKS * CHUNK
    _, band_a = _band_geometry(A_PREV_CHUNKS)
    diag = np.arange(Q_TILE + win_a - 1) - (win_a - 1) + A_PREV_CHUNKS * CHUNK
    rel_idx = np.clip(diag, -A_MAX_REL, A_MAX_REL) + A_MAX_REL
    dist_b, band_b = _band_geometry(B_PREV_CHUNKS)
    b_lane_heads = np.arange(B_Q_HEADS).reshape(B_KV_HEADS, B_GROUP).T.reshape(-1)
    alibi = -_alibi_slopes(B_Q_HEADS)[:, None, None] * np.abs(dist_b).astype(np.float32)[None]
    bias_b = jnp.asarray(np.where(band_b[None], (alibi * np.float32(LOG2_E))[b_lane_heads],
                                  np.float32(NEG_INF)))

    row_f32 = jax.ShapeDtypeStruct((n_rows, D_MODEL), F32)
    in_hbm = pl.BlockSpec(memory_space=pl.ANY)
    act_scratch = pltpu.VMEM((ROW_TILE, D_FF), BF16)

    h = x.reshape(n_rows, D_MODEL)
    for i in range(depth):
        vec = lambda g: g[i].astype(F32).reshape(1, -1)
        qk_gain = jnp.concatenate([
            jnp.tile(a_q_norm[i], A_HEADS) * q_scale, jnp.tile(a_k_norm[i], A_HEADS),
            jnp.ones((A_WIDTH,), F32),
            jnp.tile(b_q_norm[i], B_Q_HEADS) * q_scale, jnp.tile(b_k_norm[i], B_KV_HEADS),
            jnp.ones((B_KV_WIDTH,), F32)]).astype(F32).reshape(1, IN_COLS)
        qb_lo = 3 * A_WIDTH
        w_qb = w_in[i][:, qb_lo:qb_lo + B_Q_WIDTH].reshape(D_MODEL, B_KV_HEADS, B_GROUP, HEAD_DIM)
        w_in_i = jnp.concatenate([w_in[i][:, :qb_lo], w_qb.transpose(0, 2, 1, 3).reshape(D_MODEL, B_Q_WIDTH),
                                  w_in[i][:, qb_lo + B_Q_WIDTH:]], axis=1)
        w_proj_b_i = (w_proj_b[i].reshape(B_KV_HEADS, B_GROUP, HEAD_DIM, D_MODEL)
                      .transpose(1, 0, 2, 3).reshape(B_Q_WIDTH, D_MODEL))

        h1, qkv = _dense_call(
            _ffn1_proj_kernel, n_rows,
            in_specs=[_row_tiled(D_MODEL), _resident((1, D_MODEL)), in_hbm, in_hbm,
                      _resident((1, D_MODEL)), _resident((D_MODEL, IN_COLS)), _resident((1, IN_COLS))],
            out_specs=[_row_tiled(D_MODEL), _row_tiled(IN_COLS)],
            out_shape=[row_f32, jax.ShapeDtypeStruct((n_rows, IN_COLS), BF16)],
            scratch_shapes=[act_scratch, pltpu.VMEM((ROW_TILE, IN_COLS), F32)] + _ffn_weight_scratch(),
            name="ffn1_proj",
        )(h, vec(ffn1_norm), ffn1_w_gu[i].astype(F32), ffn1_w_down[i].astype(F32),
          vec(mix_norm), w_in_i.astype(BF16), qk_gain)

        qkv3 = qkv.reshape(b, s, IN_COLS)
        rel = _toeplitz((a_rel_bias[i].astype(F32) * np.float32(LOG2_E))[:, rel_idx], Q_TILE, win_a)
        bias_a = jnp.where(band_a[None], rel, NEG_INF)
        ya = _band_attention(qkv3, bias_a, None, q_col=0, k_col=1, v_col=2,
                             n_heads=A_HEADS, n_kv=A_HEADS, n_prev=A_PREV_CHUNKS, name="mixer_a")
        kb_col = (3 * A_WIDTH + B_Q_WIDTH) // B_KV_WIDTH
        sinks = (b_sinks[i].astype(F32) * np.float32(LOG2_E))[b_lane_heads]
        yb = _band_attention(qkv3, bias_b, sinks, q_col=3, k_col=kb_col, v_col=kb_col + 1,
                             n_heads=B_Q_HEADS, n_kv=B_KV_HEADS, n_prev=B_PREV_CHUNKS, name="mixer_b")

        h = _dense_call(
            _merge_ffn2_ple_kernel, n_rows,
            in_specs=[_row_tiled(D_MODEL), _row_tiled(A_WIDTH), _row_tiled(B_Q_WIDTH), _row_tiled(PLE_DIM),
                      _resident((1, D_MODEL)), _resident((D_MODEL, 2 * D_MODEL)), _resident((A_WIDTH, D_MODEL)),
                      _resident((B_Q_WIDTH, D_MODEL)), _resident((D_MODEL, D_MODEL)),
                      _resident((1, D_MODEL)), in_hbm, in_hbm,
                      _resident((1, D_MODEL)), _resident((D_MODEL, D_MODEL)), _resident((PLE_DIM, D_MODEL))],
            out_specs=_row_tiled(D_MODEL), out_shape=row_f32,
            scratch_shapes=[act_scratch] + _ffn_weight_scratch(), name="merge_ffn2_ple",
        )(h1, ya.reshape(n_rows, A_WIDTH), yb.reshape(n_rows, B_Q_WIDTH), p[i].reshape(n_rows, PLE_DIM),
          vec(mix_norm), w_gate[i].astype(BF16), w_proj_a[i].astype(BF16), w_proj_b_i.astype(BF16),
          w_out[i].astype(BF16), vec(ffn2_norm), ffn2_w_gu[i].astype(F32), ffn2_w_down[i].astype(F32),
          vec(ple_norm), w_ple_gate[i].astype(BF16), w_ple_proj[i].astype(BF16))
    return h.reshape(b, s, D_MODEL)
```

```python
import functools

import numpy as np
import jax
import jax.numpy as jnp
from jax import lax
from jax.experimental import pallas as pl
from jax.experimental.pallas import tpu as pltpu

D_MODEL = 1024
D_FF = 2816
PLE_DIM = 256
CHUNK = 64
HEAD_DIM = 64
A_HEADS = 8
A_PREV_CHUNKS = 8
A_MAX_REL = 128
B_Q_HEADS = 8
B_KV_HEADS = 2
B_GROUP = B_Q_HEADS // B_KV_HEADS
B_PREV_CHUNKS = 2
A_WIDTH = A_HEADS * HEAD_DIM
B_Q_WIDTH = B_Q_HEADS * HEAD_DIM
B_KV_WIDTH = B_KV_HEADS * HEAD_DIM
IN_COLS = 3 * A_WIDTH + B_Q_WIDTH + 2 * B_KV_WIDTH
EPS = 1e-6
NEG_INF = -1e30
LOG2_E = 1.4426950408889634

V7X_MXU_DIM = 256
V7X_LANES = 128
V7X_VMEM_BYTES = 64 * 1024 * 1024
VMEM_LIMIT_BYTES = V7X_VMEM_BYTES - 8 * 1024 * 1024

ROW_TILE = 512
FF_CHUNK = V7X_MXU_DIM
UP_STAGE_ROWS = 128
DOWN_STAGE_ROWS = 256
Q_TILE = 2 * CHUNK
SCORE_LOOKAHEAD = 1

BF16 = jnp.bfloat16
F32 = jnp.float32


def _dot(a, b):
    return jnp.dot(a, b, preferred_element_type=F32)


def _sigmoid(x):
    return 1.0 / (1.0 + jnp.exp(-x))


def _rms_norm(h, gain):
    ms = jnp.mean(h * h, axis=-1, keepdims=True)
    return (h * lax.rsqrt(ms + EPS)) * gain


def _swiglu(xn, wgu_ref, wd_ref, act_ref):
    for j in range(D_FF // FF_CHUNK):
        lo = j * FF_CHUNK
        g = _dot(xn, wgu_ref[:, lo:lo + FF_CHUNK])
        u = _dot(xn, wgu_ref[:, D_FF + lo:D_FF + lo + FF_CHUNK])
        act_ref[:, lo:lo + FF_CHUNK] = ((g * _sigmoid(g)) * u).astype(BF16)
    return _dot(act_ref[...], wd_ref[...])


def _fetch_ffn_weights(wgu_hbm, wd_hbm, wgu_ref, wd_ref, stage_gu_ref, stage_d_ref, sem_ref):
    @pl.when(pl.program_id(0) == 0)
    def _():
        for w, (src, dst, stage) in enumerate(((wgu_hbm, wgu_ref, stage_gu_ref),
                                               (wd_hbm, wd_ref, stage_d_ref))):
            rows = stage.shape[1]
            n_chunks = src.shape[0] // rows

            def copy(c, src=src, stage=stage, rows=rows, w=w):
                return pltpu.make_async_copy(src.at[pl.ds(c * rows, rows)], stage.at[c % 2],
                                             sem_ref.at[w, c % 2])

            copy(0).start()
            for c in range(n_chunks):
                if c + 1 < n_chunks:
                    copy(c + 1).start()
                copy(c).wait()
                dst[c * rows:(c + 1) * rows, :] = stage[c % 2].astype(BF16)


def _ffn_weight_scratch():
    return [pltpu.VMEM((D_MODEL, 2 * D_FF), BF16), pltpu.VMEM((D_FF, D_MODEL), BF16),
            pltpu.VMEM((2, UP_STAGE_ROWS, 2 * D_FF), F32), pltpu.VMEM((2, DOWN_STAGE_ROWS, D_MODEL), F32),
            pltpu.SemaphoreType.DMA((2, 2))]


def _head_pair_norm(x, gain):
    first_head = lax.broadcasted_iota(jnp.int32, x.shape, 1) < HEAD_DIM
    sq = x * x
    s0 = jnp.sum(jnp.where(first_head, sq, 0.0), axis=-1, keepdims=True)
    s1 = jnp.sum(jnp.where(first_head, 0.0, sq), axis=-1, keepdims=True)
    ms = jnp.where(first_head, s0, s1) * (1.0 / HEAD_DIM)
    return (x * lax.rsqrt(ms + EPS)) * gain


_QKA_COLS = (0, 2 * A_WIDTH)
_VA_COLS = (2 * A_WIDTH, 3 * A_WIDTH)
_QB_COLS = (3 * A_WIDTH, 3 * A_WIDTH + B_Q_WIDTH)
_KB_COLS = (_QB_COLS[1], _QB_COLS[1] + B_KV_WIDTH)
_KVB_COLS = (_QB_COLS[1], IN_COLS)


def _ffn1_proj_kernel(x_ref, g1_ref, wgu_hbm, wd_hbm, gm_ref, win_ref, qkg_ref,
                      h1_ref, qkv_ref, act_ref, y_ref, wgu_ref, wd_ref, *fetch_scratch):
    _fetch_ffn_weights(wgu_hbm, wd_hbm, wgu_ref, wd_ref, *fetch_scratch)
    x = x_ref[...]
    h1 = x + 0.5 * _swiglu(_rms_norm(x, g1_ref[...]).astype(BF16), wgu_ref, wd_ref, act_ref)
    h1_ref[...] = h1
    u = _rms_norm(h1, gm_ref[...]).astype(BF16)
    for lo, hi in (_QKA_COLS, _QB_COLS):
        y_ref[:, lo:hi] = _dot(u, win_ref[:, lo:hi])
    half = ROW_TILE // 2
    lo, hi = _KVB_COLS
    y_ref[:half, lo:hi] = _dot(u[:half], win_ref[:, lo:hi])
    y_ref[half:, lo:hi] = _dot(u[half:], win_ref[:, lo:hi])
    lo, hi = _VA_COLS
    y_ref[:, lo:hi] = _dot(u, win_ref[:, lo:hi])

    for first, last in (_QKA_COLS, _QB_COLS, _KB_COLS):
        for lo in range(first, last, V7X_LANES):
            cols = slice(lo, lo + V7X_LANES)
            qkv_ref[:, cols] = _head_pair_norm(y_ref[:, cols], qkg_ref[:, cols]).astype(BF16)
    for lo, hi in (_VA_COLS, (_KB_COLS[1], IN_COLS)):
        qkv_ref[:, lo:hi] = y_ref[:, lo:hi].astype(BF16)


def _merge_ffn2_ple_kernel(h1_ref, ya_ref, yb_ref, p_ref, gm_ref, wg_ref, wa_ref, wb_ref, wo_ref,
                           g2_ref, wgu_hbm, wd_hbm, gp_ref, wpg_ref, wpe_ref, out_ref, act_ref,
                           wgu_ref, wd_ref, *fetch_scratch):
    _fetch_ffn_weights(wgu_hbm, wd_hbm, wgu_ref, wd_ref, *fetch_scratch)
    h1 = h1_ref[...]
    u = _rms_norm(h1, gm_ref[...]).astype(BF16)
    ga = _sigmoid(_dot(u, wg_ref[:, :D_MODEL]))
    gb = _sigmoid(_dot(u, wg_ref[:, D_MODEL:]))
    merged = ga * _dot(ya_ref[...], wa_ref[...]) + gb * _dot(yb_ref[...], wb_ref[...])
    h2 = h1 + _dot(merged.astype(BF16), wo_ref[...])
    h3 = h2 + 0.5 * _swiglu(_rms_norm(h2, g2_ref[...]).astype(BF16), wgu_ref, wd_ref, act_ref)
    gate = _sigmoid(_dot(_rms_norm(h3, gp_ref[...]).astype(BF16), wpg_ref[...]))
    out_ref[...] = h3 + gate * _dot(p_ref[...].astype(BF16), wpe_ref[...])


def _band_attention_kernel(*refs, n_pairs, kv_pairs, pad, has_sink):
    if has_sink:
        sink_ref, q_ref, k_ref, v_ref, bias_ref, o_ref = refs
    else:
        q_ref, k_ref, v_ref, bias_ref, o_ref = refs
    seq = k_ref.shape[1]
    pair = 2 * HEAD_DIM
    first_head = lax.broadcasted_iota(jnp.int32, (Q_TILE, pair), 1) < HEAD_DIM
    first_rows = lax.broadcasted_iota(jnp.int32, (2 * Q_TILE, 1), 0) < Q_TILE
    ones = jnp.ones((Q_TILE + pad, pair), BF16)

    def key_rows(tile):
        return max(tile * Q_TILE - pad, 0), (tile + 1) * Q_TILE

    def scores(tile, pp):
        kp = pp % kv_pairs
        lo, hi = key_rows(tile)
        q = q_ref[0, tile * Q_TILE:(tile + 1) * Q_TILE, pair * pp:pair * (pp + 1)].astype(F32)
        q2 = jnp.concatenate([jnp.where(first_head, q, 0.0), jnp.where(first_head, 0.0, q)], axis=0)
        k = k_ref[0, lo:hi, pair * kp:pair * (kp + 1)]
        s = lax.dot_general(q2.astype(BF16), k, (((1,), (1,)), ((), ())), preferred_element_type=F32)
        clipped = Q_TILE + pad - (hi - lo)
        return s + bias_ref[2 * pp:2 * pp + 2, :, clipped:].reshape(2 * Q_TILE, hi - lo)

    def finish(tile, pp, s):
        kp = pp % kv_pairs
        lo, hi = key_rows(tile)
        v = jnp.concatenate([v_ref[0, lo:hi, pair * kp:pair * (kp + 1)], ones[:hi - lo]], axis=1)
        m = jnp.max(s, axis=-1, keepdims=True)
        if has_sink:
            sink = jnp.where(first_rows, sink_ref[2 * pp], sink_ref[2 * pp + 1])
            m = jnp.maximum(m, sink)
        o2 = _dot(jnp.exp2(s - m).astype(BF16), v)
        denom = o2[:, pair:]
        if has_sink:
            denom = denom + jnp.exp2(sink - m)
        o2 = o2[:, :pair] / denom
        o_ref[0, tile * Q_TILE:(tile + 1) * Q_TILE, pair * pp:pair * (pp + 1)] = (
            jnp.where(first_head, o2[:Q_TILE], o2[Q_TILE:]).astype(BF16))

    units = [(tile, pp) for tile in range(seq // Q_TILE) for pp in range(n_pairs)]
    pending = [scores(*u) for u in units[:SCORE_LOOKAHEAD]]
    for n, u in enumerate(units):
        if n + SCORE_LOOKAHEAD < len(units):
            pending.append(scores(*units[n + SCORE_LOOKAHEAD]))
        finish(*u, pending.pop(0))


def _resident(shape):
    return pl.BlockSpec(shape, lambda *_: (0,) * len(shape), pipeline_mode=pl.Buffered(1))


def _row_tiled(width):
    return pl.BlockSpec((ROW_TILE, width), lambda i: (i, 0))


def _dense_call(kernel, n_rows, in_specs, out_specs, out_shape, scratch_shapes, name):
    return pl.pallas_call(
        kernel,
        grid=(n_rows // ROW_TILE,),
        in_specs=in_specs,
        out_specs=out_specs,
        out_shape=out_shape,
        scratch_shapes=scratch_shapes,
        compiler_params=pltpu.CompilerParams(
            dimension_semantics=("arbitrary",), vmem_limit_bytes=VMEM_LIMIT_BYTES),
        name=name,
    )


def _band_attention(qkv, bias, sinks, *, q_col, k_col, v_col, n_heads, n_kv, n_prev, name):
    b, s, _ = qkv.shape
    pad = n_prev * CHUNK
    qw, kw = n_heads * HEAD_DIM, n_kv * HEAD_DIM
    has_sink = sinks is not None
    in_specs = [
        pl.BlockSpec((1, s, qw), lambda bi: (bi, 0, q_col)),
        pl.BlockSpec((1, s, kw), lambda bi: (bi, 0, k_col)),
        pl.BlockSpec((1, s, kw), lambda bi: (bi, 0, v_col)),
        _resident(bias.shape),
    ]
    args = [qkv, qkv, qkv, bias]
    if has_sink:
        in_specs.insert(0, pl.BlockSpec(memory_space=pltpu.SMEM))
        args.insert(0, sinks)
    return pl.pallas_call(
        functools.partial(_band_attention_kernel, n_pairs=n_heads // 2, kv_pairs=n_kv // 2,
                          pad=pad, has_sink=has_sink),
        grid=(b,),
        in_specs=in_specs,
        out_specs=pl.BlockSpec((1, s, qw), lambda bi: (bi, 0, 0)),
        out_shape=jax.ShapeDtypeStruct((b, s, qw), BF16),
        compiler_params=pltpu.CompilerParams(
            dimension_semantics=("arbitrary",), vmem_limit_bytes=VMEM_LIMIT_BYTES),
        name=name,
    )(*args)


def _band_geometry(n_prev):
    pad = n_prev * CHUNK
    assert pad % Q_TILE == 0
    i = np.arange(Q_TILE)[:, None]
    j = np.arange(Q_TILE + pad)[None, :]
    dist = i + pad - j
    q_chunk, k_chunk = i // CHUNK, j // CHUNK
    in_band = (k_chunk >= q_chunk) & (k_chunk <= q_chunk + n_prev)
    return dist, in_band


def _take_runs(table, idx):
    pieces, i = [], 0
    while i < len(idx):
        j = i
        while j + 1 < len(idx) and idx[j + 1] == idx[j] + 1:
            j += 1
        if j == i:
            while j + 1 < len(idx) and idx[j + 1] == idx[i]:
                j += 1
            pieces.append(jnp.broadcast_to(table[:, idx[i]:idx[i] + 1], (table.shape[0], j - i + 1)))
        else:
            pieces.append(table[:, idx[i]:idx[j] + 1])
        i = j + 1
    return jnp.concatenate(pieces, axis=1)


def _toeplitz(g_rev, n_rows, n_cols):
    length = n_rows + n_cols - 1
    period = jnp.pad(g_rev, ((0, 0), (0, 1)))
    skew = jnp.tile(period, (1, n_rows))[:, :n_rows * length].reshape(-1, n_rows, length)
    return skew[:, :, n_rows - 1:]


def _alibi_slopes(n_heads):
    return np.array([2.0 ** (-8.0 * (h + 1) / n_heads) for h in range(n_heads)], dtype=np.float32)


def kernel(x, p, ffn1_norm, ffn1_w_gu, ffn1_w_down, mix_norm, w_in, a_q_norm, a_k_norm, a_rel_bias, b_q_norm, b_k_norm, b_sinks, w_gate, w_proj_a, w_proj_b, w_out, ffn2_norm, ffn2_w_gu, ffn2_w_down, ple_norm, w_ple_gate, w_ple_proj):
    b, s, d = x.shape
    depth = ffn1_w_gu.shape[0]
    assert d == D_MODEL and s % Q_TILE == 0 and (b * s) % ROW_TILE == 0
    n_rows = b * s
    q_scale = np.float32(HEAD_DIM ** -0.5 * LOG2_E)

    win_a = Q_TILE + A_PREV_CHUNKS * CHUNK
    _, band_a = _band_geometry(A_PREV_CHUNKS)
    diag = np.arange(Q_TILE + win_a - 1) - (win_a - 1) + A_PREV_CHUNKS * CHUNK
    rel_idx = np.clip(diag, -A_MAX_REL, A_MAX_REL) + A_MAX_REL
    dist_b, band_b = _band_geometry(B_PREV_CHUNKS)
    b_lane_heads = np.arange(B_Q_HEADS).reshape(B_KV_HEADS, B_GROUP).T.reshape(-1)
    alibi = -_alibi_slopes(B_Q_HEADS)[:, None, None] * np.abs(dist_b).astype(np.float32)[None]
    bias_b = jnp.asarray(np.where(band_b[None], (alibi * np.float32(LOG2_E))[b_lane_heads],
                                  np.float32(NEG_INF)))

    row_f32 = jax.ShapeDtypeStruct((n_rows, D_MODEL), F32)
    in_hbm = pl.BlockSpec(memory_space=pl.ANY)
    act_scratch = pltpu.VMEM((ROW_TILE, D_FF), BF16)

    h = x.reshape(n_rows, D_MODEL)
    for i in range(depth):
        vec = lambda g: g[i].astype(F32).reshape(1, -1)
        qk_gain = jnp.concatenate([
            jnp.tile(a_q_norm[i], A_HEADS) * q_scale, jnp.tile(a_k_norm[i], A_HEADS),
            jnp.ones((A_WIDTH,), F32),
            jnp.tile(b_q_norm[i], B_Q_HEADS) * q_scale, jnp.tile(b_k_norm[i], B_KV_HEADS),
            jnp.ones((B_KV_WIDTH,), F32)]).astype(F32).reshape(1, IN_COLS)
        qb_lo = 3 * A_WIDTH
        w_in_bf = w_in[i].astype(BF16)
        w_qb = w_in_bf[:, qb_lo:qb_lo + B_Q_WIDTH].reshape(D_MODEL, B_KV_HEADS, B_GROUP, HEAD_DIM)
        w_in_i = jnp.concatenate([w_in_bf[:, :qb_lo], w_qb.transpose(0, 2, 1, 3).reshape(D_MODEL, B_Q_WIDTH),
                                  w_in_bf[:, qb_lo + B_Q_WIDTH:]], axis=1)
        w_proj_b_i = (w_proj_b[i].astype(BF16).reshape(B_KV_HEADS, B_GROUP, HEAD_DIM, D_MODEL)
                      .transpose(1, 0, 2, 3).reshape(B_Q_WIDTH, D_MODEL))

        h1, qkv = _dense_call(
            _ffn1_proj_kernel, n_rows,
            in_specs=[_row_tiled(D_MODEL), _resident((1, D_MODEL)), in_hbm, in_hbm,
                      _resident((1, D_MODEL)), _resident((D_MODEL, IN_COLS)), _resident((1, IN_COLS))],
            out_specs=[_row_tiled(D_MODEL), _row_tiled(IN_COLS)],
            out_shape=[row_f32, jax.ShapeDtypeStruct((n_rows, IN_COLS), BF16)],
            scratch_shapes=[act_scratch, pltpu.VMEM((ROW_TILE, IN_COLS), F32)] + _ffn_weight_scratch(),
            name="ffn1_proj",
        )(h, vec(ffn1_norm), ffn1_w_gu[i].astype(F32), ffn1_w_down[i].astype(F32),
          vec(mix_norm), w_in_i, qk_gain)

        qkv3 = qkv.reshape(b, s, IN_COLS)
        rel_diag = _take_runs(a_rel_bias[i].astype(F32) * np.float32(LOG2_E), rel_idx)
        rel = _toeplitz(rel_diag[:, ::-1], Q_TILE, win_a)
        bias_a = jnp.where(band_a[None], rel, NEG_INF)
        ya = _band_attention(qkv3, bias_a, None, q_col=0, k_col=1, v_col=2,
                             n_heads=A_HEADS, n_kv=A_HEADS, n_prev=A_PREV_CHUNKS, name="mixer_a")
        kb_col = (3 * A_WIDTH + B_Q_WIDTH) // B_KV_WIDTH
        sinks = (b_sinks[i].astype(F32) * np.float32(LOG2_E))[b_lane_heads]
        yb = _band_attention(qkv3, bias_b, sinks, q_col=3, k_col=kb_col, v_col=kb_col + 1,
                             n_heads=B_Q_HEADS, n_kv=B_KV_HEADS, n_prev=B_PREV_CHUNKS, name="mixer_b")

        h = _dense_call(
            _merge_ffn2_ple_kernel, n_rows,
            in_specs=[_row_tiled(D_MODEL), _row_tiled(A_WIDTH), _row_tiled(B_Q_WIDTH), _row_tiled(PLE_DIM),
                      _resident((1, D_MODEL)), _resident((D_MODEL, 2 * D_MODEL)), _resident((A_WIDTH, D_MODEL)),
                      _resident((B_Q_WIDTH, D_MODEL)), _resident((D_MODEL, D_MODEL)),
                      _resident((1, D_MODEL)), in_hbm, in_hbm,
                      _resident((1, D_MODEL)), _resident((D_MODEL, D_MODEL)), _resident((PLE_DIM, D_MODEL))],
            out_specs=_row_tiled(D_MODEL), out_shape=row_f32,
            scratch_shapes=[act_scratch] + _ffn_weight_scratch(), name="merge_ffn2_ple",
        )(h1, ya.reshape(n_rows, A_WIDTH), yb.reshape(n_rows, B_Q_WIDTH), p[i].reshape(n_rows, PLE_DIM),
          vec(mix_norm), w_gate[i].astype(BF16), w_proj_a[i].astype(BF16), w_proj_b_i,
          w_out[i].astype(BF16), vec(ffn2_norm), ffn2_w_gu[i].astype(F32), ffn2_w_down[i].astype(F32),
          vec(ple_norm), w_ple_gate[i].astype(BF16), w_ple_proj[i].astype(BF16))
    return h.reshape(b, s, D_MODEL)
```

```python
import functools

import numpy as np
import jax
import jax.numpy as jnp
from jax import lax
from jax.experimental import pallas as pl
from jax.experimental.pallas import tpu as pltpu

D_MODEL = 1024
D_FF = 2816
PLE_DIM = 256
CHUNK = 64
HEAD_DIM = 64
A_HEADS = 8
A_PREV_CHUNKS = 8
A_MAX_REL = 128
B_Q_HEADS = 8
B_KV_HEADS = 2
B_GROUP = B_Q_HEADS // B_KV_HEADS
B_PREV_CHUNKS = 2
A_WIDTH = A_HEADS * HEAD_DIM
B_Q_WIDTH = B_Q_HEADS * HEAD_DIM
B_KV_WIDTH = B_KV_HEADS * HEAD_DIM
IN_COLS = 3 * A_WIDTH + B_Q_WIDTH + 2 * B_KV_WIDTH
EPS = 1e-6
NEG_INF = -1e30
LOG2_E = 1.4426950408889634

V7X_MXU_DIM = 256
V7X_LANES = 128
V7X_VMEM_BYTES = 64 * 1024 * 1024
VMEM_LIMIT_BYTES = V7X_VMEM_BYTES - 8 * 1024 * 1024

ROW_TILE = 512
FF_CHUNK = V7X_MXU_DIM
UP_STAGE_ROWS = 128
DOWN_STAGE_ROWS = 256
Q_TILE = 2 * CHUNK
SCORE_LOOKAHEAD = 1

BF16 = jnp.bfloat16
F32 = jnp.float32


def _dot(a, b):
    return jnp.dot(a, b, preferred_element_type=F32)


def _sigmoid(x):
    return 1.0 / (1.0 + jnp.exp(-x))


def _rms_norm(h, gain):
    ms = jnp.mean(h * h, axis=-1, keepdims=True)
    return (h * lax.rsqrt(ms + EPS)) * gain


def _swiglu(xn, wgu_ref, wd_ref, act_ref):
    for j in range(D_FF // FF_CHUNK):
        lo = j * FF_CHUNK
        g = _dot(xn, wgu_ref[:, lo:lo + FF_CHUNK])
        u = _dot(xn, wgu_ref[:, D_FF + lo:D_FF + lo + FF_CHUNK])
        act_ref[:, lo:lo + FF_CHUNK] = ((g * _sigmoid(g)) * u).astype(BF16)
    return _dot(act_ref[...], wd_ref[...])


def _fetch_ffn_weights(wgu_hbm, wd_hbm, wgu_ref, wd_ref, stage_gu_ref, stage_d_ref, sem_ref):
    @pl.when(pl.program_id(0) == 0)
    def _():
        for w, (src, dst, stage) in enumerate(((wgu_hbm, wgu_ref, stage_gu_ref),
                                               (wd_hbm, wd_ref, stage_d_ref))):
            rows = stage.shape[1]
            n_chunks = src.shape[0] // rows

            def copy(c, src=src, stage=stage, rows=rows, w=w):
                return pltpu.make_async_copy(src.at[pl.ds(c * rows, rows)], stage.at[c % 2],
                                             sem_ref.at[w, c % 2])

            copy(0).start()
            for c in range(n_chunks):
                if c + 1 < n_chunks:
                    copy(c + 1).start()
                copy(c).wait()
                dst[c * rows:(c + 1) * rows, :] = stage[c % 2].astype(BF16)


def _ffn_weight_scratch():
    return [pltpu.VMEM((D_MODEL, 2 * D_FF), BF16), pltpu.VMEM((D_FF, D_MODEL), BF16),
            pltpu.VMEM((2, UP_STAGE_ROWS, 2 * D_FF), F32), pltpu.VMEM((2, DOWN_STAGE_ROWS, D_MODEL), F32),
            pltpu.SemaphoreType.DMA((2, 2))]


def _head_pair_norm(x, gain):
    first_head = lax.broadcasted_iota(jnp.int32, x.shape, 1) < HEAD_DIM
    sq = x * x
    s0 = jnp.sum(jnp.where(first_head, sq, 0.0), axis=-1, keepdims=True)
    s1 = jnp.sum(jnp.where(first_head, 0.0, sq), axis=-1, keepdims=True)
    ms = jnp.where(first_head, s0, s1) * (1.0 / HEAD_DIM)
    return (x * lax.rsqrt(ms + EPS)) * gain


_QKA_COLS = (0, 2 * A_WIDTH)
_VA_COLS = (2 * A_WIDTH, 3 * A_WIDTH)
_QB_COLS = (3 * A_WIDTH, 3 * A_WIDTH + B_Q_WIDTH)
_KB_COLS = (_QB_COLS[1], _QB_COLS[1] + B_KV_WIDTH)
_KVB_COLS = (_QB_COLS[1], IN_COLS)


def _ffn1_proj_kernel(x_ref, g1_ref, wgu_hbm, wd_hbm, gm_ref, win_ref, qkg_ref,
                      h1_ref, qkv_ref, act_ref, y_ref, wgu_ref, wd_ref, *fetch_scratch):
    _fetch_ffn_weights(wgu_hbm, wd_hbm, wgu_ref, wd_ref, *fetch_scratch)
    x = x_ref[...]
    h1 = x + 0.5 * _swiglu(_rms_norm(x, g1_ref[...]).astype(BF16), wgu_ref, wd_ref, act_ref)
    h1_ref[...] = h1
    u = _rms_norm(h1, gm_ref[...]).astype(BF16)
    for lo, hi in (_QKA_COLS, _QB_COLS):
        y_ref[:, lo:hi] = _dot(u, win_ref[:, lo:hi])
    half = ROW_TILE // 2
    lo, hi = _KVB_COLS
    y_ref[:half, lo:hi] = _dot(u[:half], win_ref[:, lo:hi])
    y_ref[half:, lo:hi] = _dot(u[half:], win_ref[:, lo:hi])
    lo, hi = _VA_COLS
    y_ref[:, lo:hi] = _dot(u, win_ref[:, lo:hi])

    for first, last in (_QKA_COLS, _QB_COLS, _KB_COLS):
        for lo in range(first, last, V7X_LANES):
            cols = slice(lo, lo + V7X_LANES)
            qkv_ref[:, cols] = _head_pair_norm(y_ref[:, cols], qkg_ref[:, cols]).astype(BF16)
    for lo, hi in (_VA_COLS, (_KB_COLS[1], IN_COLS)):
        qkv_ref[:, lo:hi] = y_ref[:, lo:hi].astype(BF16)


def _merge_ffn2_ple_kernel(h1_ref, ya_ref, yb_ref, p_ref, gm_ref, wg_ref, wa_ref, wb_ref, wo_ref,
                           g2_ref, wgu_hbm, wd_hbm, gp_ref, wpg_ref, wpe_ref, out_ref, act_ref,
                           wgu_ref, wd_ref, *fetch_scratch):
    _fetch_ffn_weights(wgu_hbm, wd_hbm, wgu_ref, wd_ref, *fetch_scratch)
    h1 = h1_ref[...]
    u = _rms_norm(h1, gm_ref[...]).astype(BF16)
    ga = _sigmoid(_dot(u, wg_ref[:, :D_MODEL]))
    gb = _sigmoid(_dot(u, wg_ref[:, D_MODEL:]))
    merged = ga * _dot(ya_ref[...], wa_ref[...]) + gb * _dot(yb_ref[...], wb_ref[...])
    h2 = h1 + _dot(merged.astype(BF16), wo_ref[...])
    h3 = h2 + 0.5 * _swiglu(_rms_norm(h2, g2_ref[...]).astype(BF16), wgu_ref, wd_ref, act_ref)
    gate = _sigmoid(_dot(_rms_norm(h3, gp_ref[...]).astype(BF16), wpg_ref[...]))
    out_ref[...] = h3 + gate * _dot(p_ref[...].astype(BF16), wpe_ref[...])


def _band_attention_kernel(*refs, n_pairs, kv_pairs, pad, has_sink):
    if has_sink:
        sink_ref, q_ref, k_ref, v_ref, bias_ref, o_ref = refs
    else:
        q_ref, k_ref, v_ref, bias_ref, o_ref = refs
    seq = k_ref.shape[1]
    pair = 2 * HEAD_DIM
    first_head = lax.broadcasted_iota(jnp.int32, (Q_TILE, pair), 1) < HEAD_DIM
    first_rows = lax.broadcasted_iota(jnp.int32, (2 * Q_TILE, 1), 0) < Q_TILE
    ones = jnp.ones((Q_TILE + pad, pair), BF16)

    def key_rows(tile):
        return max(tile * Q_TILE - pad, 0), (tile + 1) * Q_TILE

    def scores(tile, pp):
        kp = pp % kv_pairs
        lo, hi = key_rows(tile)
        q = q_ref[0, tile * Q_TILE:(tile + 1) * Q_TILE, pair * pp:pair * (pp + 1)].astype(F32)
        q2 = jnp.concatenate([jnp.where(first_head, q, 0.0), jnp.where(first_head, 0.0, q)], axis=0)
        k = k_ref[0, lo:hi, pair * kp:pair * (kp + 1)]
        s = lax.dot_general(q2.astype(BF16), k, (((1,), (1,)), ((), ())), preferred_element_type=F32)
        clipped = Q_TILE + pad - (hi - lo)
        return s + bias_ref[2 * pp:2 * pp + 2, :, clipped:].reshape(2 * Q_TILE, hi - lo)

    def finish(tile, pp, s):
        kp = pp % kv_pairs
        lo, hi = key_rows(tile)
        v = jnp.concatenate([v_ref[0, lo:hi, pair * kp:pair * (kp + 1)], ones[:hi - lo]], axis=1)
        m = jnp.max(s, axis=-1, keepdims=True)
        if has_sink:
            sink = jnp.where(first_rows, sink_ref[2 * pp], sink_ref[2 * pp + 1])
            m = jnp.maximum(m, sink)
        o2 = _dot(jnp.exp2(s - m).astype(BF16), v)
        denom = o2[:, pair:]
        if has_sink:
            denom = denom + jnp.exp2(sink - m)
        o2 = o2[:, :pair] / denom
        o_ref[0, tile * Q_TILE:(tile + 1) * Q_TILE, pair * pp:pair * (pp + 1)] = (
            jnp.where(first_head, o2[:Q_TILE], o2[Q_TILE:]).astype(BF16))

    units = [(tile, pp) for tile in range(seq // Q_TILE) for pp in range(n_pairs)]
    pending = [scores(*u) for u in units[:SCORE_LOOKAHEAD]]
    for n, u in enumerate(units):
        if n + SCORE_LOOKAHEAD < len(units):
            pending.append(scores(*units[n + SCORE_LOOKAHEAD]))
        finish(*u, pending.pop(0))


def _resident(shape):
    return pl.BlockSpec(shape, lambda *_: (0,) * len(shape), pipeline_mode=pl.Buffered(1))


def _row_tiled(width):
    return pl.BlockSpec((ROW_TILE, width), lambda i: (i, 0))


def _dense_call(kernel, n_rows, in_specs, out_specs, out_shape, scratch_shapes, name):
    return pl.pallas_call(
        kernel,
        grid=(n_rows // ROW_TILE,),
        in_specs=in_specs,
        out_specs=out_specs,
        out_shape=out_shape,
        scratch_shapes=scratch_shapes,
        compiler_params=pltpu.CompilerParams(
            dimension_semantics=("arbitrary",), vmem_limit_bytes=VMEM_LIMIT_BYTES),
        name=name,
    )


def _band_attention(qkv, bias, sinks, *, q_col, k_col, v_col, n_heads, n_kv, n_prev, name):
    b, s, _ = qkv.shape
    pad = n_prev * CHUNK
    qw, kw = n_heads * HEAD_DIM, n_kv * HEAD_DIM
    has_sink = sinks is not None
    in_specs = [
        pl.BlockSpec((1, s, qw), lambda bi: (bi, 0, q_col)),
        pl.BlockSpec((1, s, kw), lambda bi: (bi, 0, k_col)),
        pl.BlockSpec((1, s, kw), lambda bi: (bi, 0, v_col)),
        _resident(bias.shape),
    ]
    args = [qkv, qkv, qkv, bias]
    if has_sink:
        in_specs.insert(0, pl.BlockSpec(memory_space=pltpu.SMEM))
        args.insert(0, sinks)
    return pl.pallas_call(
        functools.partial(_band_attention_kernel, n_pairs=n_heads // 2, kv_pairs=n_kv // 2,
                          pad=pad, has_sink=has_sink),
        grid=(b,),
        in_specs=in_specs,
        out_specs=pl.BlockSpec((1, s, qw), lambda bi: (bi, 0, 0)),
        out_shape=jax.ShapeDtypeStruct((b, s, qw), BF16),
        compiler_params=pltpu.CompilerParams(
            dimension_semantics=("arbitrary",), vmem_limit_bytes=VMEM_LIMIT_BYTES),
        name=name,
    )(*args)


def _band_geometry(n_prev):
    pad = n_prev * CHUNK
    assert pad % Q_TILE == 0
    i = np.arange(Q_TILE)[:, None]
    j = np.arange(Q_TILE + pad)[None, :]
    dist = i + pad - j
    q_chunk, k_chunk = i // CHUNK, j // CHUNK
    in_band = (k_chunk >= q_chunk) & (k_chunk <= q_chunk + n_prev)
    return dist, in_band


def _take_runs(table, idx):
    pieces, i = [], 0
    while i < len(idx):
        j = i
        while j + 1 < len(idx) and idx[j + 1] == idx[j] + 1:
            j += 1
        if j == i:
            while j + 1 < len(idx) and idx[j + 1] == idx[i]:
                j += 1
            pieces.append(jnp.broadcast_to(table[:, idx[i]:idx[i] + 1], (table.shape[0], j - i + 1)))
        else:
            pieces.append(table[:, idx[i]:idx[j] + 1])
        i = j + 1
    return jnp.concatenate(pieces, axis=1)


def _toeplitz(g, n_rows, n_cols):
    length = n_rows + n_cols - 1
    skew = jnp.tile(g, (1, n_rows + 1))[:, :n_rows * (length + 1)].reshape(-1, n_rows, length + 1)
    return skew[:, :, :n_cols][:, :, ::-1]


def _alibi_slopes(n_heads):
    return np.array([2.0 ** (-8.0 * (h + 1) / n_heads) for h in range(n_heads)], dtype=np.float32)


def kernel(x, p, ffn1_norm, ffn1_w_gu, ffn1_w_down, mix_norm, w_in, a_q_norm, a_k_norm, a_rel_bias, b_q_norm, b_k_norm, b_sinks, w_gate, w_proj_a, w_proj_b, w_out, ffn2_norm, ffn2_w_gu, ffn2_w_down, ple_norm, w_ple_gate, w_ple_proj):
    b, s, d = x.shape
    depth = ffn1_w_gu.shape[0]
    assert d == D_MODEL and s % Q_TILE == 0 and (b * s) % ROW_TILE == 0
    n_rows = b * s
    q_scale = np.float32(HEAD_DIM ** -0.5 * LOG2_E)

    win_a = Q_TILE + A_PREV_CHUNKS * CHUNK
    _, band_a = _band_geometry(A_PREV_CHUNKS)
    diag = np.arange(Q_TILE + win_a - 1) - (win_a - 1) + A_PREV_CHUNKS * CHUNK
    rel_idx = np.clip(diag, -A_MAX_REL, A_MAX_REL) + A_MAX_REL
    dist_b, band_b = _band_geometry(B_PREV_CHUNKS)
    b_lane_heads = np.arange(B_Q_HEADS).reshape(B_KV_HEADS, B_GROUP).T.reshape(-1)
    alibi = -_alibi_slopes(B_Q_HEADS)[:, None, None] * np.abs(dist_b).astype(np.float32)[None]
    bias_b = jnp.asarray(np.where(band_b[None], (alibi * np.float32(LOG2_E))[b_lane_heads],
                                  np.float32(NEG_INF)))

    row_f32 = jax.ShapeDtypeStruct((n_rows, D_MODEL), F32)
    in_hbm = pl.BlockSpec(memory_space=pl.ANY)
    act_scratch = pltpu.VMEM((ROW_TILE, D_FF), BF16)

    h = x.reshape(n_rows, D_MODEL)
    for i in range(depth):
        vec = lambda g: g[i].astype(F32).reshape(1, -1)
        qk_gain = jnp.concatenate([
            jnp.tile(a_q_norm[i], A_HEADS) * q_scale, jnp.tile(a_k_norm[i], A_HEADS),
            jnp.ones((A_WIDTH,), F32),
            jnp.tile(b_q_norm[i], B_Q_HEADS) * q_scale, jnp.tile(b_k_norm[i], B_KV_HEADS),
            jnp.ones((B_KV_WIDTH,), F32)]).astype(F32).reshape(1, IN_COLS)
        qb_lo = 3 * A_WIDTH
        w_in_bf = w_in[i].astype(BF16)
        w_qb = w_in_bf[:, qb_lo:qb_lo + B_Q_WIDTH].reshape(D_MODEL, B_KV_HEADS, B_GROUP, HEAD_DIM)
        w_in_i = jnp.concatenate([w_in_bf[:, :qb_lo], w_qb.transpose(0, 2, 1, 3).reshape(D_MODEL, B_Q_WIDTH),
                                  w_in_bf[:, qb_lo + B_Q_WIDTH:]], axis=1)
        w_proj_b_i = (w_proj_b[i].astype(BF16).reshape(B_KV_HEADS, B_GROUP, HEAD_DIM, D_MODEL)
                      .transpose(1, 0, 2, 3).reshape(B_Q_WIDTH, D_MODEL))

        h1, qkv = _dense_call(
            _ffn1_proj_kernel, n_rows,
            in_specs=[_row_tiled(D_MODEL), _resident((1, D_MODEL)), in_hbm, in_hbm,
                      _resident((1, D_MODEL)), _resident((D_MODEL, IN_COLS)), _resident((1, IN_COLS))],
            out_specs=[_row_tiled(D_MODEL), _row_tiled(IN_COLS)],
            out_shape=[row_f32, jax.ShapeDtypeStruct((n_rows, IN_COLS), BF16)],
            scratch_shapes=[act_scratch, pltpu.VMEM((ROW_TILE, IN_COLS), F32)] + _ffn_weight_scratch(),
            name="ffn1_proj",
        )(h, vec(ffn1_norm), ffn1_w_gu[i].astype(F32), ffn1_w_down[i].astype(F32),
          vec(mix_norm), w_in_i, qk_gain)

        qkv3 = qkv.reshape(b, s, IN_COLS)
        rel_diag = _take_runs(a_rel_bias[i].astype(F32) * np.float32(LOG2_E), rel_idx)
        rel = _toeplitz(rel_diag, Q_TILE, win_a)
        bias_a = jnp.where(band_a[None], rel, NEG_INF)
        ya = _band_attention(qkv3, bias_a, None, q_col=0, k_col=1, v_col=2,
                             n_heads=A_HEADS, n_kv=A_HEADS, n_prev=A_PREV_CHUNKS, name="mixer_a")
        kb_col = (3 * A_WIDTH + B_Q_WIDTH) // B_KV_WIDTH
        sinks = (b_sinks[i].astype(F32) * np.float32(LOG2_E))[b_lane_heads]
        yb = _band_attention(qkv3, bias_b, sinks, q_col=3, k_col=kb_col, v_col=kb_col + 1,
                             n_heads=B_Q_HEADS, n_kv=B_KV_HEADS, n_prev=B_PREV_CHUNKS, name="mixer_b")

        h = _dense_call(
            _merge_ffn2_ple_kernel, n_rows,
            in_specs=[_row_tiled(D_MODEL), _row_tiled(A_WIDTH), _row_tiled(B_Q_WIDTH), _row_tiled(PLE_DIM),
                      _resident((1, D_MODEL)), _resident((D_MODEL, 2 * D_MODEL)), _resident((A_WIDTH, D_MODEL)),
                      _resident((B_Q_WIDTH, D_MODEL)), _resident((D_MODEL, D_MODEL)),
                      _resident((1, D_MODEL)), in_hbm, in_hbm,
                      _resident((1, D_MODEL)), _resident((D_MODEL, D_MODEL)), _resident((PLE_DIM, D_MODEL))],
            out_specs=_row_tiled(D_MODEL), out_shape=row_f32,
            scratch_shapes=[act_scratch] + _ffn_weight_scratch(), name="merge_ffn2_ple",
        )(h1, ya.reshape(n_rows, A_WIDTH), yb.reshape(n_rows, B_Q_WIDTH), p[i].reshape(n_rows, PLE_DIM),
          vec(mix_norm), w_gate[i].astype(BF16), w_proj_a[i].astype(BF16), w_proj_b_i,
          w_out[i].astype(BF16), vec(ffn2_norm), ffn2_w_gu[i].astype(F32), ffn2_w_down[i].astype(F32),
          vec(ple_norm), w_ple_gate[i].astype(BF16), w_ple_proj[i].astype(BF16))
    return h.reshape(b, s, D_MODEL)
```

```python
import functools

import numpy as np
import jax
import jax.numpy as jnp
from jax import lax
from jax.experimental import pallas as pl
from jax.experimental.pallas import tpu as pltpu

D_MODEL = 1024
D_FF = 2816
PLE_DIM = 256
CHUNK = 64
HEAD_DIM = 64
A_HEADS = 8
A_PREV_CHUNKS = 8
A_MAX_REL = 128
B_Q_HEADS = 8
B_KV_HEADS = 2
B_GROUP = B_Q_HEADS // B_KV_HEADS
B_PREV_CHUNKS = 2
A_WIDTH = A_HEADS * HEAD_DIM
B_Q_WIDTH = B_Q_HEADS * HEAD_DIM
B_KV_WIDTH = B_KV_HEADS * HEAD_DIM
IN_COLS = 3 * A_WIDTH + B_Q_WIDTH + 2 * B_KV_WIDTH
EPS = 1e-6
NEG_INF = -1e30
LOG2_E = 1.4426950408889634

V7X_MXU_DIM = 256
V7X_LANES = 128
V7X_VMEM_BYTES = 64 * 1024 * 1024
VMEM_LIMIT_BYTES = V7X_VMEM_BYTES - 8 * 1024 * 1024

ROW_TILE = 512
FF_CHUNK = V7X_MXU_DIM
UP_STAGE_ROWS = 128
DOWN_STAGE_ROWS = 256
Q_TILE = 2 * CHUNK
SCORE_LOOKAHEAD = 1

BF16 = jnp.bfloat16
F32 = jnp.float32


def _dot(a, b):
    return jnp.dot(a, b, preferred_element_type=F32)


def _sigmoid(x):
    return 1.0 / (1.0 + jnp.exp(-x))


def _rms_norm(h, gain):
    ms = jnp.mean(h * h, axis=-1, keepdims=True)
    return (h * lax.rsqrt(ms + EPS)) * gain


def _swiglu(xn, wgu_ref, wd_ref, act_ref):
    for j in range(D_FF // FF_CHUNK):
        lo = j * FF_CHUNK
        g = _dot(xn, wgu_ref[:, lo:lo + FF_CHUNK])
        u = _dot(xn, wgu_ref[:, D_FF + lo:D_FF + lo + FF_CHUNK])
        act_ref[:, lo:lo + FF_CHUNK] = ((g * _sigmoid(g)) * u).astype(BF16)
    return _dot(act_ref[...], wd_ref[...])


def _fetch_ffn_weights(wgu_hbm, wd_hbm, wgu_ref, wd_ref, stage_gu_ref, stage_d_ref, sem_ref):
    @pl.when(pl.program_id(0) == 0)
    def _():
        for w, (src, dst, stage) in enumerate(((wgu_hbm, wgu_ref, stage_gu_ref),
                                               (wd_hbm, wd_ref, stage_d_ref))):
            rows = stage.shape[1]
            n_chunks = src.shape[0] // rows

            def copy(c, src=src, stage=stage, rows=rows, w=w):
                return pltpu.make_async_copy(src.at[pl.ds(c * rows, rows)], stage.at[c % 2],
                                             sem_ref.at[w, c % 2])

            copy(0).start()
            for c in range(n_chunks):
                if c + 1 < n_chunks:
                    copy(c + 1).start()
                copy(c).wait()
                dst[c * rows:(c + 1) * rows, :] = stage[c % 2].astype(BF16)


def _ffn_weight_scratch():
    return [pltpu.VMEM((D_MODEL, 2 * D_FF), BF16), pltpu.VMEM((D_FF, D_MODEL), BF16),
            pltpu.VMEM((2, UP_STAGE_ROWS, 2 * D_FF), F32), pltpu.VMEM((2, DOWN_STAGE_ROWS, D_MODEL), F32),
            pltpu.SemaphoreType.DMA((2, 2))]


def _head_pair_norm(x, gain):
    first_head = lax.broadcasted_iota(jnp.int32, x.shape, 1) < HEAD_DIM
    sq = x * x
    s0 = jnp.sum(jnp.where(first_head, sq, 0.0), axis=-1, keepdims=True)
    s1 = jnp.sum(jnp.where(first_head, 0.0, sq), axis=-1, keepdims=True)
    ms = jnp.where(first_head, s0, s1) * (1.0 / HEAD_DIM)
    return (x * lax.rsqrt(ms + EPS)) * gain


_QKA_COLS = (0, 2 * A_WIDTH)
_VA_COLS = (2 * A_WIDTH, 3 * A_WIDTH)
_QB_COLS = (3 * A_WIDTH, 3 * A_WIDTH + B_Q_WIDTH)
_KB_COLS = (_QB_COLS[1], _QB_COLS[1] + B_KV_WIDTH)
_KVB_COLS = (_QB_COLS[1], IN_COLS)


def _ffn1_proj_kernel(x_ref, g1_ref, wgu_hbm, wd_hbm, gm_ref, win_ref, qkg_ref,
                      h1_ref, qkv_ref, act_ref, y_ref, wgu_ref, wd_ref, *fetch_scratch):
    _fetch_ffn_weights(wgu_hbm, wd_hbm, wgu_ref, wd_ref, *fetch_scratch)
    x = x_ref[...]
    h1 = x + 0.5 * _swiglu(_rms_norm(x, g1_ref[...]).astype(BF16), wgu_ref, wd_ref, act_ref)
    h1_ref[...] = h1
    u = _rms_norm(h1, gm_ref[...]).astype(BF16)
    for lo, hi in (_QKA_COLS, _QB_COLS):
        y_ref[:, lo:hi] = _dot(u, win_ref[:, lo:hi])
    half = ROW_TILE // 2
    lo, hi = _KVB_COLS
    y_ref[:half, lo:hi] = _dot(u[:half], win_ref[:, lo:hi])
    y_ref[half:, lo:hi] = _dot(u[half:], win_ref[:, lo:hi])
    lo, hi = _VA_COLS
    y_ref[:, lo:hi] = _dot(u, win_ref[:, lo:hi])

    for first, last in (_QKA_COLS, _QB_COLS, _KB_COLS):
        for lo in range(first, last, V7X_LANES):
            cols = slice(lo, lo + V7X_LANES)
            qkv_ref[:, cols] = _head_pair_norm(y_ref[:, cols], qkg_ref[:, cols]).astype(BF16)
    for lo, hi in (_VA_COLS, (_KB_COLS[1], IN_COLS)):
        qkv_ref[:, lo:hi] = y_ref[:, lo:hi].astype(BF16)


def _merge_ffn2_ple_kernel(h1_ref, ya_ref, yb_ref, p_ref, gm_ref, wg_ref, wa_ref, wb_ref, wo_ref,
                           g2_ref, wgu_hbm, wd_hbm, gp_ref, wpg_ref, wpe_ref, out_ref, act_ref,
                           wgu_ref, wd_ref, *fetch_scratch):
    _fetch_ffn_weights(wgu_hbm, wd_hbm, wgu_ref, wd_ref, *fetch_scratch)
    h1 = h1_ref[...]
    u = _rms_norm(h1, gm_ref[...]).astype(BF16)
    ga = _sigmoid(_dot(u, wg_ref[:, :D_MODEL]))
    gb = _sigmoid(_dot(u, wg_ref[:, D_MODEL:]))
    merged = ga * _dot(ya_ref[...], wa_ref[...]) + gb * _dot(yb_ref[...], wb_ref[...])
    h2 = h1 + _dot(merged.astype(BF16), wo_ref[...])
    h3 = h2 + 0.5 * _swiglu(_rms_norm(h2, g2_ref[...]).astype(BF16), wgu_ref, wd_ref, act_ref)
    gate = _sigmoid(_dot(_rms_norm(h3, gp_ref[...]).astype(BF16), wpg_ref[...]))
    out_ref[...] = h3 + gate * _dot(p_ref[...].astype(BF16), wpe_ref[...])


def _band_attention_kernel(*refs, n_pairs, kv_pairs, pad, has_sink):
    if has_sink:
        sink_ref, q_ref, k_ref, v_ref, bias_ref, o_ref = refs
    else:
        q_ref, k_ref, v_ref, bias_ref, o_ref = refs
    seq = k_ref.shape[1]
    pair = 2 * HEAD_DIM
    first_head = lax.broadcasted_iota(jnp.int32, (Q_TILE, pair), 1) < HEAD_DIM
    first_rows = lax.broadcasted_iota(jnp.int32, (2 * Q_TILE, 1), 0) < Q_TILE
    ones = jnp.ones((Q_TILE + pad, pair), BF16)

    def key_rows(tile):
        return max(tile * Q_TILE - pad, 0), (tile + 1) * Q_TILE

    def scores(tile, pp):
        kp = pp % kv_pairs
        lo, hi = key_rows(tile)
        q = q_ref[0, tile * Q_TILE:(tile + 1) * Q_TILE, pair * pp:pair * (pp + 1)].astype(F32)
        q2 = jnp.concatenate([jnp.where(first_head, q, 0.0), jnp.where(first_head, 0.0, q)], axis=0)
        k = k_ref[0, lo:hi, pair * kp:pair * (kp + 1)]
        s = lax.dot_general(q2.astype(BF16), k, (((1,), (1,)), ((), ())), preferred_element_type=F32)
        clipped = Q_TILE + pad - (hi - lo)
        return s + bias_ref[2 * pp:2 * pp + 2, :, clipped:].reshape(2 * Q_TILE, hi - lo)

    def finish(tile, pp, s):
        kp = pp % kv_pairs
        lo, hi = key_rows(tile)
        v = jnp.concatenate([v_ref[0, lo:hi, pair * kp:pair * (kp + 1)], ones[:hi - lo]], axis=1)
        m = jnp.max(s, axis=-1, keepdims=True)
        if has_sink:
            sink = jnp.where(first_rows, sink_ref[2 * pp], sink_ref[2 * pp + 1])
            m = jnp.maximum(m, sink)
        o2 = _dot(jnp.exp2(s - m).astype(BF16), v)
        denom = o2[:, pair:]
        if has_sink:
            denom = denom + jnp.exp2(sink - m)
        o2 = o2[:, :pair] / denom
        o_ref[0, tile * Q_TILE:(tile + 1) * Q_TILE, pair * pp:pair * (pp + 1)] = (
            jnp.where(first_head, o2[:Q_TILE], o2[Q_TILE:]).astype(BF16))

    units = [(tile, pp) for tile in range(seq // Q_TILE) for pp in range(n_pairs)]
    pending = [scores(*u) for u in units[:SCORE_LOOKAHEAD]]
    for n, u in enumerate(units):
        if n + SCORE_LOOKAHEAD < len(units):
            pending.append(scores(*units[n + SCORE_LOOKAHEAD]))
        finish(*u, pending.pop(0))


def _resident(shape):
    return pl.BlockSpec(shape, lambda *_: (0,) * len(shape), pipeline_mode=pl.Buffered(1))


def _row_tiled(width):
    return pl.BlockSpec((ROW_TILE, width), lambda i: (i, 0))


def _dense_call(kernel, n_rows, in_specs, out_specs, out_shape, scratch_shapes, name):
    return pl.pallas_call(
        kernel,
        grid=(n_rows // ROW_TILE,),
        in_specs=in_specs,
        out_specs=out_specs,
        out_shape=out_shape,
        scratch_shapes=scratch_shapes,
        compiler_params=pltpu.CompilerParams(
            dimension_semantics=("arbitrary",), vmem_limit_bytes=VMEM_LIMIT_BYTES),
        name=name,
    )


def _band_attention(qkv, bias, sinks, *, q_col, k_col, v_col, n_heads, n_kv, n_prev, name):
    b, s, _ = qkv.shape
    pad = n_prev * CHUNK
    qw, kw = n_heads * HEAD_DIM, n_kv * HEAD_DIM
    has_sink = sinks is not None
    in_specs = [
        pl.BlockSpec((1, s, qw), lambda bi: (bi, 0, q_col)),
        pl.BlockSpec((1, s, kw), lambda bi: (bi, 0, k_col)),
        pl.BlockSpec((1, s, kw), lambda bi: (bi, 0, v_col)),
        _resident(bias.shape),
    ]
    args = [qkv, qkv, qkv, bias]
    if has_sink:
        in_specs.insert(0, pl.BlockSpec(memory_space=pltpu.SMEM))
        args.insert(0, sinks)
    return pl.pallas_call(
        functools.partial(_band_attention_kernel, n_pairs=n_heads // 2, kv_pairs=n_kv // 2,
                          pad=pad, has_sink=has_sink),
        grid=(b,),
        in_specs=in_specs,
        out_specs=pl.BlockSpec((1, s, qw), lambda bi: (bi, 0, 0)),
        out_shape=jax.ShapeDtypeStruct((b, s, qw), BF16),
        compiler_params=pltpu.CompilerParams(
            dimension_semantics=("arbitrary",), vmem_limit_bytes=VMEM_LIMIT_BYTES),
        name=name,
    )(*args)


def _band_geometry(n_prev):
    pad = n_prev * CHUNK
    assert pad % Q_TILE == 0
    i = np.arange(Q_TILE)[:, None]
    j = np.arange(Q_TILE + pad)[None, :]
    dist = i + pad - j
    q_chunk, k_chunk = i // CHUNK, j // CHUNK
    in_band = (k_chunk >= q_chunk) & (k_chunk <= q_chunk + n_prev)
    return dist, in_band


def _take_runs(table, idx):
    pieces, i = [], 0
    while i < len(idx):
        j = i
        while j + 1 < len(idx) and idx[j + 1] == idx[j] + 1:
            j += 1
        if j == i:
            while j + 1 < len(idx) and idx[j + 1] == idx[i]:
                j += 1
            pieces.append(jnp.broadcast_to(table[:, idx[i]:idx[i] + 1], (table.shape[0], j - i + 1)))
        else:
            pieces.append(table[:, idx[i]:idx[j] + 1])
        i = j + 1
    return jnp.concatenate(pieces, axis=1)


def _toeplitz(g, n_rows, n_cols):
    length = n_rows + n_cols - 1
    skew = jnp.tile(g, (1, n_rows + 1))[:, :n_rows * (length + 1)].reshape(-1, n_rows, length + 1)
    return skew[:, :, :n_cols][:, :, ::-1]


def _alibi_slopes(n_heads):
    return np.array([2.0 ** (-8.0 * (h + 1) / n_heads) for h in range(n_heads)], dtype=np.float32)


def kernel(x, p, ffn1_norm, ffn1_w_gu, ffn1_w_down, mix_norm, w_in, a_q_norm, a_k_norm, a_rel_bias, b_q_norm, b_k_norm, b_sinks, w_gate, w_proj_a, w_proj_b, w_out, ffn2_norm, ffn2_w_gu, ffn2_w_down, ple_norm, w_ple_gate, w_ple_proj):
    b, s, d = x.shape
    depth = ffn1_w_gu.shape[0]
    assert d == D_MODEL and s % Q_TILE == 0 and (b * s) % ROW_TILE == 0
    n_rows = b * s
    q_scale = np.float32(HEAD_DIM ** -0.5 * LOG2_E)

    win_a = Q_TILE + A_PREV_CHUNKS * CHUNK
    _, band_a = _band_geometry(A_PREV_CHUNKS)
    diag = np.arange(Q_TILE + win_a - 1) - (win_a - 1) + A_PREV_CHUNKS * CHUNK
    rel_idx = np.clip(diag, -A_MAX_REL, A_MAX_REL) + A_MAX_REL
    dist_b, band_b = _band_geometry(B_PREV_CHUNKS)
    b_lane_heads = np.arange(B_Q_HEADS).reshape(B_KV_HEADS, B_GROUP).T.reshape(-1)
    alibi = -_alibi_slopes(B_Q_HEADS)[:, None, None] * np.abs(dist_b).astype(np.float32)[None]
    bias_b = jnp.asarray(np.where(band_b[None], (alibi * np.float32(LOG2_E))[b_lane_heads],
                                  np.float32(NEG_INF)))

    row_f32 = jax.ShapeDtypeStruct((n_rows, D_MODEL), F32)
    in_hbm = pl.BlockSpec(memory_space=pl.ANY)
    act_scratch = pltpu.VMEM((ROW_TILE, D_FF), BF16)

    h = x.reshape(n_rows, D_MODEL)
    for i in range(depth):
        vec = lambda g: g[i].astype(F32).reshape(1, -1)
        qk_gain = jnp.concatenate([
            jnp.tile(a_q_norm[i], A_HEADS) * q_scale, jnp.tile(a_k_norm[i], A_HEADS),
            jnp.ones((A_WIDTH,), F32),
            jnp.tile(b_q_norm[i], B_Q_HEADS) * q_scale, jnp.tile(b_k_norm[i], B_KV_HEADS),
            jnp.ones((B_KV_WIDTH,), F32)]).astype(F32).reshape(1, IN_COLS)
        qb_lo = 3 * A_WIDTH
        w_qb = w_in[i][:, qb_lo:qb_lo + B_Q_WIDTH].reshape(D_MODEL, B_KV_HEADS, B_GROUP, HEAD_DIM)
        w_in_i = jnp.concatenate([w_in[i][:, :qb_lo], w_qb.transpose(0, 2, 1, 3).reshape(D_MODEL, B_Q_WIDTH),
                                  w_in[i][:, qb_lo + B_Q_WIDTH:]], axis=1).astype(BF16)
        w_proj_b_i = (w_proj_b[i].reshape(B_KV_HEADS, B_GROUP, HEAD_DIM, D_MODEL)
                      .transpose(1, 0, 2, 3).reshape(B_Q_WIDTH, D_MODEL)).astype(BF16)

        h1, qkv = _dense_call(
            _ffn1_proj_kernel, n_rows,
            in_specs=[_row_tiled(D_MODEL), _resident((1, D_MODEL)), in_hbm, in_hbm,
                      _resident((1, D_MODEL)), _resident((D_MODEL, IN_COLS)), _resident((1, IN_COLS))],
            out_specs=[_row_tiled(D_MODEL), _row_tiled(IN_COLS)],
            out_shape=[row_f32, jax.ShapeDtypeStruct((n_rows, IN_COLS), BF16)],
            scratch_shapes=[act_scratch, pltpu.VMEM((ROW_TILE, IN_COLS), F32)] + _ffn_weight_scratch(),
            name="ffn1_proj",
        )(h, vec(ffn1_norm), ffn1_w_gu[i].astype(F32), ffn1_w_down[i].astype(F32),
          vec(mix_norm), w_in_i, qk_gain)

        qkv3 = qkv.reshape(b, s, IN_COLS)
        rel_diag = _take_runs(a_rel_bias[i].astype(F32) * np.float32(LOG2_E), rel_idx)
        rel = _toeplitz(rel_diag, Q_TILE, win_a)
        bias_a = jnp.where(band_a[None], rel, NEG_INF)
        ya = _band_attention(qkv3, bias_a, None, q_col=0, k_col=1, v_col=2,
                             n_heads=A_HEADS, n_kv=A_HEADS, n_prev=A_PREV_CHUNKS, name="mixer_a")
        kb_col = (3 * A_WIDTH + B_Q_WIDTH) // B_KV_WIDTH
        sinks = (b_sinks[i].astype(F32) * np.float32(LOG2_E))[b_lane_heads]
        yb = _band_attention(qkv3, bias_b, sinks, q_col=3, k_col=kb_col, v_col=kb_col + 1,
                             n_heads=B_Q_HEADS, n_kv=B_KV_HEADS, n_prev=B_PREV_CHUNKS, name="mixer_b")

        h = _dense_call(
            _merge_ffn2_ple_kernel, n_rows,
            in_specs=[_row_tiled(D_MODEL), _row_tiled(A_WIDTH), _row_tiled(B_Q_WIDTH), _row_tiled(PLE_DIM),
                      _resident((1, D_MODEL)), _resident((D_MODEL, 2 * D_MODEL)), _resident((A_WIDTH, D_MODEL)),
                      _resident((B_Q_WIDTH, D_MODEL)), _resident((D_MODEL, D_MODEL)),
                      _resident((1, D_MODEL)), in_hbm, in_hbm,
                      _resident((1, D_MODEL)), _resident((D_MODEL, D_MODEL)), _resident((PLE_DIM, D_MODEL))],
            out_specs=_row_tiled(D_MODEL), out_shape=row_f32,
            scratch_shapes=[act_scratch] + _ffn_weight_scratch(), name="merge_ffn2_ple",
        )(h1, ya.reshape(n_rows, A_WIDTH), yb.reshape(n_rows, B_Q_WIDTH), p[i].reshape(n_rows, PLE_DIM),
          vec(mix_norm), w_gate[i].astype(BF16), w_proj_a[i].astype(BF16), w_proj_b_i,
          w_out[i].astype(BF16), vec(ffn2_norm), ffn2_w_gu[i].astype(F32), ffn2_w_down[i].astype(F32),
          vec(ple_norm), w_ple_gate[i].astype(BF16), w_ple_proj[i].astype(BF16))
    return h.reshape(b, s, D_MODEL)
```

```python
import functools

import numpy as np
import jax
import jax.numpy as jnp
from jax import lax
from jax.experimental import pallas as pl
from jax.experimental.pallas import tpu as pltpu

D_MODEL = 1024
D_FF = 2816
PLE_DIM = 256
CHUNK = 64
HEAD_DIM = 64
A_HEADS = 8
A_PREV_CHUNKS = 8
A_MAX_REL = 128
B_Q_HEADS = 8
B_KV_HEADS = 2
B_GROUP = B_Q_HEADS // B_KV_HEADS
B_PREV_CHUNKS = 2
A_WIDTH = A_HEADS * HEAD_DIM
B_Q_WIDTH = B_Q_HEADS * HEAD_DIM
B_KV_WIDTH = B_KV_HEADS * HEAD_DIM
IN_COLS = 3 * A_WIDTH + B_Q_WIDTH + 2 * B_KV_WIDTH
EPS = 1e-6
NEG_INF = -1e30
LOG2_E = 1.4426950408889634

V7X_MXU_DIM = 256
V7X_LANES = 128
V7X_VMEM_BYTES = 64 * 1024 * 1024
VMEM_LIMIT_BYTES = V7X_VMEM_BYTES - 8 * 1024 * 1024

ROW_TILE = 512
FF_CHUNK = V7X_MXU_DIM
UP_STAGE_ROWS = 128
DOWN_STAGE_ROWS = 256
Q_TILE = 2 * CHUNK
SCORE_LOOKAHEAD = 1

BF16 = jnp.bfloat16
F32 = jnp.float32


def _dot(a, b):
    return jnp.dot(a, b, preferred_element_type=F32)


def _sigmoid(x):
    return 1.0 / (1.0 + jnp.exp(-x))


def _rms_norm(h, gain):
    ms = jnp.mean(h * h, axis=-1, keepdims=True)
    return (h * lax.rsqrt(ms + EPS)) * gain


def _swiglu(xn, wgu_ref, wd_ref, act_ref):
    for j in range(D_FF // FF_CHUNK):
        lo = j * FF_CHUNK
        g = _dot(xn, wgu_ref[:, lo:lo + FF_CHUNK])
        u = _dot(xn, wgu_ref[:, D_FF + lo:D_FF + lo + FF_CHUNK])
        act_ref[:, lo:lo + FF_CHUNK] = ((g * _sigmoid(g)) * u).astype(BF16)
    return _dot(act_ref[...], wd_ref[...])


def _fetch_ffn_weights(wgu_hbm, wd_hbm, wgu_ref, wd_ref, stage_gu_ref, stage_d_ref, sem_ref):
    @pl.when(pl.program_id(0) == 0)
    def _():
        for w, (src, dst, stage) in enumerate(((wgu_hbm, wgu_ref, stage_gu_ref),
                                               (wd_hbm, wd_ref, stage_d_ref))):
            rows = stage.shape[1]
            n_chunks = src.shape[0] // rows

            def copy(c, src=src, stage=stage, rows=rows, w=w):
                return pltpu.make_async_copy(src.at[pl.ds(c * rows, rows)], stage.at[c % 2],
                                             sem_ref.at[w, c % 2])

            copy(0).start(priority=0)
            for c in range(n_chunks):
                if c + 1 < n_chunks:
                    copy(c + 1).start(priority=(c + 1) % 2)
                copy(c).wait()
                dst[c * rows:(c + 1) * rows, :] = stage[c % 2].astype(BF16)


def _ffn_weight_scratch():
    return [pltpu.VMEM((D_MODEL, 2 * D_FF), BF16), pltpu.VMEM((D_FF, D_MODEL), BF16),
            pltpu.VMEM((2, UP_STAGE_ROWS, 2 * D_FF), F32), pltpu.VMEM((2, DOWN_STAGE_ROWS, D_MODEL), F32),
            pltpu.SemaphoreType.DMA((2, 2))]


def _head_pair_norm(x, gain):
    first_head = lax.broadcasted_iota(jnp.int32, x.shape, 1) < HEAD_DIM
    sq = x * x
    s0 = jnp.sum(jnp.where(first_head, sq, 0.0), axis=-1, keepdims=True)
    s1 = jnp.sum(jnp.where(first_head, 0.0, sq), axis=-1, keepdims=True)
    ms = jnp.where(first_head, s0, s1) * (1.0 / HEAD_DIM)
    return (x * lax.rsqrt(ms + EPS)) * gain


_QKA_COLS = (0, 2 * A_WIDTH)
_VA_COLS = (2 * A_WIDTH, 3 * A_WIDTH)
_QB_COLS = (3 * A_WIDTH, 3 * A_WIDTH + B_Q_WIDTH)
_KB_COLS = (_QB_COLS[1], _QB_COLS[1] + B_KV_WIDTH)
_KVB_COLS = (_QB_COLS[1], IN_COLS)


def _ffn1_proj_kernel(x_ref, g1_ref, wgu_hbm, wd_hbm, gm_ref, win_ref, qkg_ref,
                      h1_ref, qkv_ref, act_ref, y_ref, wgu_ref, wd_ref, *fetch_scratch):
    _fetch_ffn_weights(wgu_hbm, wd_hbm, wgu_ref, wd_ref, *fetch_scratch)
    x = x_ref[...]
    h1 = x + 0.5 * _swiglu(_rms_norm(x, g1_ref[...]).astype(BF16), wgu_ref, wd_ref, act_ref)
    h1_ref[...] = h1
    u = _rms_norm(h1, gm_ref[...]).astype(BF16)
    for lo, hi in (_QKA_COLS, _QB_COLS):
        y_ref[:, lo:hi] = _dot(u, win_ref[:, lo:hi])
    half = ROW_TILE // 2
    lo, hi = _KVB_COLS
    y_ref[:half, lo:hi] = _dot(u[:half], win_ref[:, lo:hi])
    y_ref[half:, lo:hi] = _dot(u[half:], win_ref[:, lo:hi])
    lo, hi = _VA_COLS
    y_ref[:, lo:hi] = _dot(u, win_ref[:, lo:hi])

    for first, last in (_QKA_COLS, _QB_COLS, _KB_COLS):
        for lo in range(first, last, V7X_LANES):
            cols = slice(lo, lo + V7X_LANES)
            qkv_ref[:, cols] = _head_pair_norm(y_ref[:, cols], qkg_ref[:, cols]).astype(BF16)
    for lo, hi in (_VA_COLS, (_KB_COLS[1], IN_COLS)):
        qkv_ref[:, lo:hi] = y_ref[:, lo:hi].astype(BF16)


def _merge_ffn2_ple_kernel(h1_ref, ya_ref, yb_ref, p_ref, gm_ref, wg_ref, wa_ref, wb_ref, wo_ref,
                           g2_ref, wgu_hbm, wd_hbm, gp_ref, wpg_ref, wpe_ref, out_ref, act_ref,
                           wgu_ref, wd_ref, *fetch_scratch):
    _fetch_ffn_weights(wgu_hbm, wd_hbm, wgu_ref, wd_ref, *fetch_scratch)
    h1 = h1_ref[...]
    u = _rms_norm(h1, gm_ref[...]).astype(BF16)
    ga = _sigmoid(_dot(u, wg_ref[:, :D_MODEL]))
    gb = _sigmoid(_dot(u, wg_ref[:, D_MODEL:]))
    merged = ga * _dot(ya_ref[...], wa_ref[...]) + gb * _dot(yb_ref[...], wb_ref[...])
    h2 = h1 + _dot(merged.astype(BF16), wo_ref[...])
    h3 = h2 + 0.5 * _swiglu(_rms_norm(h2, g2_ref[...]).astype(BF16), wgu_ref, wd_ref, act_ref)
    gate = _sigmoid(_dot(_rms_norm(h3, gp_ref[...]).astype(BF16), wpg_ref[...]))
    out_ref[...] = h3 + gate * _dot(p_ref[...].astype(BF16), wpe_ref[...])


def _band_attention_kernel(*refs, n_pairs, kv_pairs, pad, has_sink):
    if has_sink:
        sink_ref, q_ref, k_ref, v_ref, bias_ref, o_ref = refs
    else:
        q_ref, k_ref, v_ref, bias_ref, o_ref = refs
    seq = k_ref.shape[1]
    pair = 2 * HEAD_DIM
    first_head = lax.broadcasted_iota(jnp.int32, (Q_TILE, pair), 1) < HEAD_DIM
    first_rows = lax.broadcasted_iota(jnp.int32, (2 * Q_TILE, 1), 0) < Q_TILE
    ones = jnp.ones((Q_TILE + pad, pair), BF16)

    def key_rows(tile):
        return max(tile * Q_TILE - pad, 0), (tile + 1) * Q_TILE

    def scores(tile, pp):
        kp = pp % kv_pairs
        lo, hi = key_rows(tile)
        q = q_ref[0, tile * Q_TILE:(tile + 1) * Q_TILE, pair * pp:pair * (pp + 1)].astype(F32)
        q2 = jnp.concatenate([jnp.where(first_head, q, 0.0), jnp.where(first_head, 0.0, q)], axis=0)
        k = k_ref[0, lo:hi, pair * kp:pair * (kp + 1)]
        s = lax.dot_general(q2.astype(BF16), k, (((1,), (1,)), ((), ())), preferred_element_type=F32)
        clipped = Q_TILE + pad - (hi - lo)
        return s + bias_ref[2 * pp:2 * pp + 2, :, clipped:].reshape(2 * Q_TILE, hi - lo)

    def finish(tile, pp, s):
        kp = pp % kv_pairs
        lo, hi = key_rows(tile)
        v = jnp.concatenate([v_ref[0, lo:hi, pair * kp:pair * (kp + 1)], ones[:hi - lo]], axis=1)
        m = jnp.max(s, axis=-1, keepdims=True)
        if has_sink:
            sink = jnp.where(first_rows, sink_ref[2 * pp], sink_ref[2 * pp + 1])
            m = jnp.maximum(m, sink)
        o2 = _dot(jnp.exp2(s - m).astype(BF16), v)
        denom = o2[:, pair:]
        if has_sink:
            denom = denom + jnp.exp2(sink - m)
        o2 = o2[:, :pair] / denom
        o_ref[0, tile * Q_TILE:(tile + 1) * Q_TILE, pair * pp:pair * (pp + 1)] = (
            jnp.where(first_head, o2[:Q_TILE], o2[Q_TILE:]).astype(BF16))

    units = [(tile, pp) for tile in range(seq // Q_TILE) for pp in range(n_pairs)]
    pending = [scores(*u) for u in units[:SCORE_LOOKAHEAD]]
    for n, u in enumerate(units):
        if n + SCORE_LOOKAHEAD < len(units):
            pending.append(scores(*units[n + SCORE_LOOKAHEAD]))
        finish(*u, pending.pop(0))


def _resident(shape):
    return pl.BlockSpec(shape, lambda *_: (0,) * len(shape), pipeline_mode=pl.Buffered(1))


def _row_tiled(width):
    return pl.BlockSpec((ROW_TILE, width), lambda i: (i, 0))


def _dense_call(kernel, n_rows, in_specs, out_specs, out_shape, scratch_shapes, name):
    return pl.pallas_call(
        kernel,
        grid=(n_rows // ROW_TILE,),
        in_specs=in_specs,
        out_specs=out_specs,
        out_shape=out_shape,
        scratch_shapes=scratch_shapes,
        compiler_params=pltpu.CompilerParams(
            dimension_semantics=("arbitrary",), vmem_limit_bytes=VMEM_LIMIT_BYTES),
        name=name,
    )


def _band_attention(qkv, bias, sinks, *, q_col, k_col, v_col, n_heads, n_kv, n_prev, name):
    b, s, _ = qkv.shape
    pad = n_prev * CHUNK
    qw, kw = n_heads * HEAD_DIM, n_kv * HEAD_DIM
    has_sink = sinks is not None
    in_specs = [
        pl.BlockSpec((1, s, qw), lambda bi: (bi, 0, q_col)),
        pl.BlockSpec((1, s, kw), lambda bi: (bi, 0, k_col)),
        pl.BlockSpec((1, s, kw), lambda bi: (bi, 0, v_col)),
        _resident(bias.shape),
    ]
    args = [qkv, qkv, qkv, bias]
    if has_sink:
        in_specs.insert(0, pl.BlockSpec(memory_space=pltpu.SMEM))
        args.insert(0, sinks)
    return pl.pallas_call(
        functools.partial(_band_attention_kernel, n_pairs=n_heads // 2, kv_pairs=n_kv // 2,
                          pad=pad, has_sink=has_sink),
        grid=(b,),
        in_specs=in_specs,
        out_specs=pl.BlockSpec((1, s, qw), lambda bi: (bi, 0, 0)),
        out_shape=jax.ShapeDtypeStruct((b, s, qw), BF16),
        compiler_params=pltpu.CompilerParams(
            dimension_semantics=("arbitrary",), vmem_limit_bytes=VMEM_LIMIT_BYTES),
        name=name,
    )(*args)


def _band_geometry(n_prev):
    pad = n_prev * CHUNK
    assert pad % Q_TILE == 0
    i = np.arange(Q_TILE)[:, None]
    j = np.arange(Q_TILE + pad)[None, :]
    dist = i + pad - j
    q_chunk, k_chunk = i // CHUNK, j // CHUNK
    in_band = (k_chunk >= q_chunk) & (k_chunk <= q_chunk + n_prev)
    return dist, in_band


def _take_runs(table, idx):
    pieces, i = [], 0
    while i < len(idx):
        j = i
        while j + 1 < len(idx) and idx[j + 1] == idx[j] + 1:
            j += 1
        if j == i:
            while j + 1 < len(idx) and idx[j + 1] == idx[i]:
                j += 1
            pieces.append(jnp.broadcast_to(table[:, idx[i]:idx[i] + 1], (table.shape[0], j - i + 1)))
        else:
            pieces.append(table[:, idx[i]:idx[j] + 1])
        i = j + 1
    return jnp.concatenate(pieces, axis=1)


def _toeplitz(g, n_rows, n_cols):
    length = n_rows + n_cols - 1
    skew = jnp.tile(g, (1, n_rows + 1))[:, :n_rows * (length + 1)].reshape(-1, n_rows, length + 1)
    return skew[:, :, :n_cols][:, :, ::-1]


def _alibi_slopes(n_heads):
    return np.array([2.0 ** (-8.0 * (h + 1) / n_heads) for h in range(n_heads)], dtype=np.float32)


def kernel(x, p, ffn1_norm, ffn1_w_gu, ffn1_w_down, mix_norm, w_in, a_q_norm, a_k_norm, a_rel_bias, b_q_norm, b_k_norm, b_sinks, w_gate, w_proj_a, w_proj_b, w_out, ffn2_norm, ffn2_w_gu, ffn2_w_down, ple_norm, w_ple_gate, w_ple_proj):
    b, s, d = x.shape
    depth = ffn1_w_gu.shape[0]
    assert d == D_MODEL and s % Q_TILE == 0 and (b * s) % ROW_TILE == 0
    n_rows = b * s
    q_scale = np.float32(HEAD_DIM ** -0.5 * LOG2_E)

    win_a = Q_TILE + A_PREV_CHUNKS * CHUNK
    _, band_a = _band_geometry(A_PREV_CHUNKS)
    diag = np.arange(Q_TILE + win_a - 1) - (win_a - 1) + A_PREV_CHUNKS * CHUNK
    rel_idx = np.clip(diag, -A_MAX_REL, A_MAX_REL) + A_MAX_REL
    dist_b, band_b = _band_geometry(B_PREV_CHUNKS)
    b_lane_heads = np.arange(B_Q_HEADS).reshape(B_KV_HEADS, B_GROUP).T.reshape(-1)
    alibi = -_alibi_slopes(B_Q_HEADS)[:, None, None] * np.abs(dist_b).astype(np.float32)[None]
    bias_b = jnp.asarray(np.where(band_b[None], (alibi * np.float32(LOG2_E))[b_lane_heads],
                                  np.float32(NEG_INF)))

    row_f32 = jax.ShapeDtypeStruct((n_rows, D_MODEL), F32)
    in_hbm = pl.BlockSpec(memory_space=pl.ANY)
    act_scratch = pltpu.VMEM((ROW_TILE, D_FF), BF16)

    h = x.reshape(n_rows, D_MODEL)
    for i in range(depth):
        vec = lambda g: g[i].astype(F32).reshape(1, -1)
        qk_gain = jnp.concatenate([
            jnp.tile(a_q_norm[i], A_HEADS) * q_scale, jnp.tile(a_k_norm[i], A_HEADS),
            jnp.ones((A_WIDTH,), F32),
            jnp.tile(b_q_norm[i], B_Q_HEADS) * q_scale, jnp.tile(b_k_norm[i], B_KV_HEADS),
            jnp.ones((B_KV_WIDTH,), F32)]).astype(F32).reshape(1, IN_COLS)
        qb_lo = 3 * A_WIDTH
        w_in_bf = w_in[i].astype(BF16)
        w_qb = w_in_bf[:, qb_lo:qb_lo + B_Q_WIDTH].reshape(D_MODEL, B_KV_HEADS, B_GROUP, HEAD_DIM)
        w_in_i = jnp.concatenate([w_in_bf[:, :qb_lo], w_qb.transpose(0, 2, 1, 3).reshape(D_MODEL, B_Q_WIDTH),
                                  w_in_bf[:, qb_lo + B_Q_WIDTH:]], axis=1)
        w_proj_b_i = (w_proj_b[i].astype(BF16).reshape(B_KV_HEADS, B_GROUP, HEAD_DIM, D_MODEL)
                      .transpose(1, 0, 2, 3).reshape(B_Q_WIDTH, D_MODEL))

        h1, qkv = _dense_call(
            _ffn1_proj_kernel, n_rows,
            in_specs=[_row_tiled(D_MODEL), _resident((1, D_MODEL)), in_hbm, in_hbm,
                      _resident((1, D_MODEL)), _resident((D_MODEL, IN_COLS)), _resident((1, IN_COLS))],
            out_specs=[_row_tiled(D_MODEL), _row_tiled(IN_COLS)],
            out_shape=[row_f32, jax.ShapeDtypeStruct((n_rows, IN_COLS), BF16)],
            scratch_shapes=[act_scratch, pltpu.VMEM((ROW_TILE, IN_COLS), F32)] + _ffn_weight_scratch(),
            name="ffn1_proj",
        )(h, vec(ffn1_norm), ffn1_w_gu[i].astype(F32), ffn1_w_down[i].astype(F32),
          vec(mix_norm), w_in_i, qk_gain)

        qkv3 = qkv.reshape(b, s, IN_COLS)
        rel_diag = _take_runs(a_rel_bias[i].astype(F32) * np.float32(LOG2_E), rel_idx)
        rel = _toeplitz(rel_diag, Q_TILE, win_a)
        bias_a = jnp.where(band_a[None], rel, NEG_INF)
        ya = _band_attention(qkv3, bias_a, None, q_col=0, k_col=1, v_col=2,
                             n_heads=A_HEADS, n_kv=A_HEADS, n_prev=A_PREV_CHUNKS, name="mixer_a")
        kb_col = (3 * A_WIDTH + B_Q_WIDTH) // B_KV_WIDTH
        sinks = (b_sinks[i].astype(F32) * np.float32(LOG2_E))[b_lane_heads]
        yb = _band_attention(qkv3, bias_b, sinks, q_col=3, k_col=kb_col, v_col=kb_col + 1,
                             n_heads=B_Q_HEADS, n_kv=B_KV_HEADS, n_prev=B_PREV_CHUNKS, name="mixer_b")

        h = _dense_call(
            _merge_ffn2_ple_kernel, n_rows,
            in_specs=[_row_tiled(D_MODEL), _row_tiled(A_WIDTH), _row_tiled(B_Q_WIDTH), _row_tiled(PLE_DIM),
                      _resident((1, D_MODEL)), _resident((D_MODEL, 2 * D_MODEL)), _resident((A_WIDTH, D_MODEL)),
                      _resident((B_Q_WIDTH, D_MODEL)), _resident((D_MODEL, D_MODEL)),
                      _resident((1, D_MODEL)), in_hbm, in_hbm,
                      _resident((1, D_MODEL)), _resident((D_MODEL, D_MODEL)), _resident((PLE_DIM, D_MODEL))],
            out_specs=_row_tiled(D_MODEL), out_shape=row_f32,
            scratch_shapes=[act_scratch] + _ffn_weight_scratch(), name="merge_ffn2_ple",
        )(h1, ya.reshape(n_rows, A_WIDTH), yb.reshape(n_rows, B_Q_WIDTH), p[i].reshape(n_rows, PLE_DIM),
          vec(mix_norm), w_gate[i].astype(BF16), w_proj_a[i].astype(BF16), w_proj_b_i,
          w_out[i].astype(BF16), vec(ffn2_norm), ffn2_w_gu[i].astype(F32), ffn2_w_down[i].astype(F32),
          vec(ple_norm), w_ple_gate[i].astype(BF16), w_ple_proj[i].astype(BF16))
    return h.reshape(b, s, D_MODEL)
```

```python
import functools

import numpy as np
import jax
import jax.numpy as jnp
from jax import lax
from jax.experimental import pallas as pl
from jax.experimental.pallas import tpu as pltpu

D_MODEL = 1024
D_FF = 2816
PLE_DIM = 256
CHUNK = 64
HEAD_DIM = 64
A_HEADS = 8
A_PREV_CHUNKS = 8
A_MAX_REL = 128
B_Q_HEADS = 8
B_KV_HEADS = 2
B_GROUP = B_Q_HEADS // B_KV_HEADS
B_PREV_CHUNKS = 2
A_WIDTH = A_HEADS * HEAD_DIM
B_Q_WIDTH = B_Q_HEADS * HEAD_DIM
B_KV_WIDTH = B_KV_HEADS * HEAD_DIM
IN_COLS = 3 * A_WIDTH + B_Q_WIDTH + 2 * B_KV_WIDTH
EPS = 1e-6
NEG_INF = -1e30
LOG2_E = 1.4426950408889634

V7X_MXU_DIM = 256
V7X_LANES = 128
V7X_VMEM_BYTES = 64 * 1024 * 1024
VMEM_LIMIT_BYTES = V7X_VMEM_BYTES - 8 * 1024 * 1024

ROW_TILE = 512
FF_CHUNK = V7X_MXU_DIM
UP_STAGE_ROWS = 128
DOWN_STAGE_ROWS = 256
Q_TILE = 2 * CHUNK
SCORE_LOOKAHEAD = 1

BF16 = jnp.bfloat16
F32 = jnp.float32


def _dot(a, b):
    return jnp.dot(a, b, preferred_element_type=F32)


def _sigmoid(x):
    return 1.0 / (1.0 + jnp.exp(-x))


def _rms_norm(h, gain):
    ms = jnp.mean(h * h, axis=-1, keepdims=True)
    return (h * lax.rsqrt(ms + EPS)) * gain


def _swiglu(xn, xn_ref, wgu_ref, wd_ref, act_ref):
    xn_ref[...] = xn
    for j in range(D_FF // FF_CHUNK):
        lo = j * FF_CHUNK
        g = _dot(xn_ref[...], wgu_ref[:, lo:lo + FF_CHUNK])
        u = _dot(xn_ref[...], wgu_ref[:, D_FF + lo:D_FF + lo + FF_CHUNK])
        act_ref[:, lo:lo + FF_CHUNK] = ((g * _sigmoid(g)) * u).astype(BF16)
    return _dot(act_ref[...], wd_ref[...])


def _fetch_ffn_weights(wgu_hbm, wd_hbm, wgu_ref, wd_ref, stage_gu_ref, stage_d_ref, sem_ref):
    @pl.when(pl.program_id(0) == 0)
    def _():
        for w, (src, dst, stage) in enumerate(((wgu_hbm, wgu_ref, stage_gu_ref),
                                               (wd_hbm, wd_ref, stage_d_ref))):
            rows = stage.shape[1]
            n_chunks = src.shape[0] // rows

            def copy(c, src=src, stage=stage, rows=rows, w=w):
                return pltpu.make_async_copy(src.at[pl.ds(c * rows, rows)], stage.at[c % 2],
                                             sem_ref.at[w, c % 2])

            copy(0).start()
            for c in range(n_chunks):
                if c + 1 < n_chunks:
                    copy(c + 1).start()
                copy(c).wait()
                dst[c * rows:(c + 1) * rows, :] = stage[c % 2].astype(BF16)


def _ffn_weight_scratch():
    return [pltpu.VMEM((D_MODEL, 2 * D_FF), BF16), pltpu.VMEM((D_FF, D_MODEL), BF16),
            pltpu.VMEM((ROW_TILE, D_MODEL), BF16),
            pltpu.VMEM((2, UP_STAGE_ROWS, 2 * D_FF), F32), pltpu.VMEM((2, DOWN_STAGE_ROWS, D_MODEL), F32),
            pltpu.SemaphoreType.DMA((2, 2))]


def _head_pair_norm(x, gain):
    first_head = lax.broadcasted_iota(jnp.int32, x.shape, 1) < HEAD_DIM
    sq = x * x
    s0 = jnp.sum(jnp.where(first_head, sq, 0.0), axis=-1, keepdims=True)
    s1 = jnp.sum(jnp.where(first_head, 0.0, sq), axis=-1, keepdims=True)
    ms = jnp.where(first_head, s0, s1) * (1.0 / HEAD_DIM)
    return (x * lax.rsqrt(ms + EPS)) * gain


_QKA_COLS = (0, 2 * A_WIDTH)
_VA_COLS = (2 * A_WIDTH, 3 * A_WIDTH)
_QB_COLS = (3 * A_WIDTH, 3 * A_WIDTH + B_Q_WIDTH)
_KB_COLS = (_QB_COLS[1], _QB_COLS[1] + B_KV_WIDTH)
_KVB_COLS = (_QB_COLS[1], IN_COLS)


def _ffn1_proj_kernel(x_ref, g1_ref, wgu_hbm, wd_hbm, gm_ref, win_ref, qkg_ref,
                      h1_ref, qkv_ref, act_ref, y_ref, wgu_ref, wd_ref, xn_ref, *fetch_scratch):
    _fetch_ffn_weights(wgu_hbm, wd_hbm, wgu_ref, wd_ref, *fetch_scratch)
    x = x_ref[...]
    h1 = x + 0.5 * _swiglu(_rms_norm(x, g1_ref[...]).astype(BF16), xn_ref, wgu_ref, wd_ref, act_ref)
    h1_ref[...] = h1
    u = _rms_norm(h1, gm_ref[...]).astype(BF16)
    for lo, hi in (_QKA_COLS, _QB_COLS):
        y_ref[:, lo:hi] = _dot(u, win_ref[:, lo:hi])
    half = ROW_TILE // 2
    lo, hi = _KVB_COLS
    y_ref[:half, lo:hi] = _dot(u[:half], win_ref[:, lo:hi])
    y_ref[half:, lo:hi] = _dot(u[half:], win_ref[:, lo:hi])
    lo, hi = _VA_COLS
    y_ref[:, lo:hi] = _dot(u, win_ref[:, lo:hi])

    for first, last in (_QKA_COLS, _QB_COLS, _KB_COLS):
        for lo in range(first, last, V7X_LANES):
            cols = slice(lo, lo + V7X_LANES)
            qkv_ref[:, cols] = _head_pair_norm(y_ref[:, cols], qkg_ref[:, cols]).astype(BF16)
    for lo, hi in (_VA_COLS, (_KB_COLS[1], IN_COLS)):
        qkv_ref[:, lo:hi] = y_ref[:, lo:hi].astype(BF16)


def _merge_ffn2_ple_kernel(h1_ref, ya_ref, yb_ref, p_ref, gm_ref, wg_ref, wa_ref, wb_ref, wo_ref,
                           g2_ref, wgu_hbm, wd_hbm, gp_ref, wpg_ref, wpe_ref, out_ref, act_ref,
                           wgu_ref, wd_ref, xn_ref, *fetch_scratch):
    _fetch_ffn_weights(wgu_hbm, wd_hbm, wgu_ref, wd_ref, *fetch_scratch)
    h1 = h1_ref[...]
    u = _rms_norm(h1, gm_ref[...]).astype(BF16)
    ga = _sigmoid(_dot(u, wg_ref[:, :D_MODEL]))
    gb = _sigmoid(_dot(u, wg_ref[:, D_MODEL:]))
    merged = ga * _dot(ya_ref[...], wa_ref[...]) + gb * _dot(yb_ref[...], wb_ref[...])
    h2 = h1 + _dot(merged.astype(BF16), wo_ref[...])
    h3 = h2 + 0.5 * _swiglu(_rms_norm(h2, g2_ref[...]).astype(BF16), xn_ref, wgu_ref, wd_ref, act_ref)
    gate = _sigmoid(_dot(_rms_norm(h3, gp_ref[...]).astype(BF16), wpg_ref[...]))
    out_ref[...] = h3 + gate * _dot(p_ref[...].astype(BF16), wpe_ref[...])


def _band_attention_kernel(*refs, n_pairs, kv_pairs, pad, has_sink):
    if has_sink:
        sink_ref, q_ref, k_ref, v_ref, bias_ref, o_ref = refs
    else:
        q_ref, k_ref, v_ref, bias_ref, o_ref = refs
    seq = k_ref.shape[1]
    pair = 2 * HEAD_DIM
    first_head = lax.broadcasted_iota(jnp.int32, (Q_TILE, pair), 1) < HEAD_DIM
    first_rows = lax.broadcasted_iota(jnp.int32, (2 * Q_TILE, 1), 0) < Q_TILE
    ones = jnp.ones((Q_TILE + pad, pair), BF16)

    def key_rows(tile):
        return max(tile * Q_TILE - pad, 0), (tile + 1) * Q_TILE

    def scores(tile, pp):
        kp = pp % kv_pairs
        lo, hi = key_rows(tile)
        q = q_ref[0, tile * Q_TILE:(tile + 1) * Q_TILE, pair * pp:pair * (pp + 1)].astype(F32)
        q2 = jnp.concatenate([jnp.where(first_head, q, 0.0), jnp.where(first_head, 0.0, q)], axis=0)
        k = k_ref[0, lo:hi, pair * kp:pair * (kp + 1)]
        s = lax.dot_general(q2.astype(BF16), k, (((1,), (1,)), ((), ())), preferred_element_type=F32)
        clipped = Q_TILE + pad - (hi - lo)
        return s + bias_ref[2 * pp:2 * pp + 2, :, clipped:].reshape(2 * Q_TILE, hi - lo)

    def finish(tile, pp, s):
        kp = pp % kv_pairs
        lo, hi = key_rows(tile)
        v = jnp.concatenate([v_ref[0, lo:hi, pair * kp:pair * (kp + 1)], ones[:hi - lo]], axis=1)
        m = jnp.max(s, axis=-1, keepdims=True)
        if has_sink:
            sink = jnp.where(first_rows, sink_ref[2 * pp], sink_ref[2 * pp + 1])
            m = jnp.maximum(m, sink)
        o2 = _dot(jnp.exp2(s - m).astype(BF16), v)
        denom = o2[:, pair:]
        if has_sink:
            denom = denom + jnp.exp2(sink - m)
        o2 = o2[:, :pair] / denom
        o_ref[0, tile * Q_TILE:(tile + 1) * Q_TILE, pair * pp:pair * (pp + 1)] = (
            jnp.where(first_head, o2[:Q_TILE], o2[Q_TILE:]).astype(BF16))

    units = [(tile, pp) for tile in range(seq // Q_TILE) for pp in range(n_pairs)]
    pending = [scores(*u) for u in units[:SCORE_LOOKAHEAD]]
    for n, u in enumerate(units):
        if n + SCORE_LOOKAHEAD < len(units):
            pending.append(scores(*units[n + SCORE_LOOKAHEAD]))
        finish(*u, pending.pop(0))


def _resident(shape):
    return pl.BlockSpec(shape, lambda *_: (0,) * len(shape), pipeline_mode=pl.Buffered(1))


def _row_tiled(width):
    return pl.BlockSpec((ROW_TILE, width), lambda i: (i, 0))


def _dense_call(kernel, n_rows, in_specs, out_specs, out_shape, scratch_shapes, name):
    return pl.pallas_call(
        kernel,
        grid=(n_rows // ROW_TILE,),
        in_specs=in_specs,
        out_specs=out_specs,
        out_shape=out_shape,
        scratch_shapes=scratch_shapes,
        compiler_params=pltpu.CompilerParams(
            dimension_semantics=("arbitrary",), vmem_limit_bytes=VMEM_LIMIT_BYTES),
        name=name,
    )


def _band_attention(qkv, bias, sinks, *, q_col, k_col, v_col, n_heads, n_kv, n_prev, name):
    b, s, _ = qkv.shape
    pad = n_prev * CHUNK
    qw, kw = n_heads * HEAD_DIM, n_kv * HEAD_DIM
    has_sink = sinks is not None
    in_specs = [
        pl.BlockSpec((1, s, qw), lambda bi: (bi, 0, q_col)),
        pl.BlockSpec((1, s, kw), lambda bi: (bi, 0, k_col)),
        pl.BlockSpec((1, s, kw), lambda bi: (bi, 0, v_col)),
        _resident(bias.shape),
    ]
    args = [qkv, qkv, qkv, bias]
    if has_sink:
        in_specs.insert(0, pl.BlockSpec(memory_space=pltpu.SMEM))
        args.insert(0, sinks)
    return pl.pallas_call(
        functools.partial(_band_attention_kernel, n_pairs=n_heads // 2, kv_pairs=n_kv // 2,
                          pad=pad, has_sink=has_sink),
        grid=(b,),
        in_specs=in_specs,
        out_specs=pl.BlockSpec((1, s, qw), lambda bi: (bi, 0, 0)),
        out_shape=jax.ShapeDtypeStruct((b, s, qw), BF16),
        compiler_params=pltpu.CompilerParams(
            dimension_semantics=("arbitrary",), vmem_limit_bytes=VMEM_LIMIT_BYTES),
        name=name,
    )(*args)


def _band_geometry(n_prev):
    pad = n_prev * CHUNK
    assert pad % Q_TILE == 0
    i = np.arange(Q_TILE)[:, None]
    j = np.arange(Q_TILE + pad)[None, :]
    dist = i + pad - j
    q_chunk, k_chunk = i // CHUNK, j // CHUNK
    in_band = (k_chunk >= q_chunk) & (k_chunk <= q_chunk + n_prev)
    return dist, in_band


def _take_runs(table, idx):
    pieces, i = [], 0
    while i < len(idx):
        j = i
        while j + 1 < len(idx) and idx[j + 1] == idx[j] + 1:
            j += 1
        if j == i:
            while j + 1 < len(idx) and idx[j + 1] == idx[i]:
                j += 1
            pieces.append(jnp.broadcast_to(table[:, idx[i]:idx[i] + 1], (table.shape[0], j - i + 1)))
        else:
            pieces.append(table[:, idx[i]:idx[j] + 1])
        i = j + 1
    return jnp.concatenate(pieces, axis=1)


def _toeplitz(g, n_rows, n_cols):
    length = n_rows + n_cols - 1
    skew = jnp.tile(g, (1, n_rows + 1))[:, :n_rows * (length + 1)].reshape(-1, n_rows, length + 1)
    return skew[:, :, :n_cols][:, :, ::-1]


def _alibi_slopes(n_heads):
    return np.array([2.0 ** (-8.0 * (h + 1) / n_heads) for h in range(n_heads)], dtype=np.float32)


def kernel(x, p, ffn1_norm, ffn1_w_gu, ffn1_w_down, mix_norm, w_in, a_q_norm, a_k_norm, a_rel_bias, b_q_norm, b_k_norm, b_sinks, w_gate, w_proj_a, w_proj_b, w_out, ffn2_norm, ffn2_w_gu, ffn2_w_down, ple_norm, w_ple_gate, w_ple_proj):
    b, s, d = x.shape
    depth = ffn1_w_gu.shape[0]
    assert d == D_MODEL and s % Q_TILE == 0 and (b * s) % ROW_TILE == 0
    n_rows = b * s
    q_scale = np.float32(HEAD_DIM ** -0.5 * LOG2_E)

    win_a = Q_TILE + A_PREV_CHUNKS * CHUNK
    _, band_a = _band_geometry(A_PREV_CHUNKS)
    diag = np.arange(Q_TILE + win_a - 1) - (win_a - 1) + A_PREV_CHUNKS * CHUNK
    rel_idx = np.clip(diag, -A_MAX_REL, A_MAX_REL) + A_MAX_REL
    dist_b, band_b = _band_geometry(B_PREV_CHUNKS)
    b_lane_heads = np.arange(B_Q_HEADS).reshape(B_KV_HEADS, B_GROUP).T.reshape(-1)
    alibi = -_alibi_slopes(B_Q_HEADS)[:, None, None] * np.abs(dist_b).astype(np.float32)[None]
    bias_b = jnp.asarray(np.where(band_b[None], (alibi * np.float32(LOG2_E))[b_lane_heads],
                                  np.float32(NEG_INF)))

    row_f32 = jax.ShapeDtypeStruct((n_rows, D_MODEL), F32)
    in_hbm = pl.BlockSpec(memory_space=pl.ANY)
    act_scratch = pltpu.VMEM((ROW_TILE, D_FF), BF16)

    h = x.reshape(n_rows, D_MODEL)
    for i in range(depth):
        vec = lambda g: g[i].astype(F32).reshape(1, -1)
        qk_gain = jnp.concatenate([
            jnp.tile(a_q_norm[i], A_HEADS) * q_scale, jnp.tile(a_k_norm[i], A_HEADS),
            jnp.ones((A_WIDTH,), F32),
            jnp.tile(b_q_norm[i], B_Q_HEADS) * q_scale, jnp.tile(b_k_norm[i], B_KV_HEADS),
            jnp.ones((B_KV_WIDTH,), F32)]).astype(F32).reshape(1, IN_COLS)
        qb_lo = 3 * A_WIDTH
        w_in_bf = w_in[i].astype(BF16)
        w_qb = w_in_bf[:, qb_lo:qb_lo + B_Q_WIDTH].reshape(D_MODEL, B_KV_HEADS, B_GROUP, HEAD_DIM)
        w_in_i = jnp.concatenate([w_in_bf[:, :qb_lo], w_qb.transpose(0, 2, 1, 3).reshape(D_MODEL, B_Q_WIDTH),
                                  w_in_bf[:, qb_lo + B_Q_WIDTH:]], axis=1)
        w_proj_b_i = (w_proj_b[i].astype(BF16).reshape(B_KV_HEADS, B_GROUP, HEAD_DIM, D_MODEL)
                      .transpose(1, 0, 2, 3).reshape(B_Q_WIDTH, D_MODEL))

        h1, qkv = _dense_call(
            _ffn1_proj_kernel, n_rows,
            in_specs=[_row_tiled(D_MODEL), _resident((1, D_MODEL)), in_hbm, in_hbm,
                      _resident((1, D_MODEL)), _resident((D_MODEL, IN_COLS)), _resident((1, IN_COLS))],
            out_specs=[_row_tiled(D_MODEL), _row_tiled(IN_COLS)],
            out_shape=[row_f32, jax.ShapeDtypeStruct((n_rows, IN_COLS), BF16)],
            scratch_shapes=[act_scratch, pltpu.VMEM((ROW_TILE, IN_COLS), F32)] + _ffn_weight_scratch(),
            name="ffn1_proj",
        )(h, vec(ffn1_norm), ffn1_w_gu[i].astype(F32), ffn1_w_down[i].astype(F32),
          vec(mix_norm), w_in_i, qk_gain)

        qkv3 = qkv.reshape(b, s, IN_COLS)
        rel_diag = _take_runs(a_rel_bias[i].astype(F32) * np.float32(LOG2_E), rel_idx)
        rel = _toeplitz(rel_diag, Q_TILE, win_a)
        bias_a = jnp.where(band_a[None], rel, NEG_INF)
        ya = _band_attention(qkv3, bias_a, None, q_col=0, k_col=1, v_col=2,
                             n_heads=A_HEADS, n_kv=A_HEADS, n_prev=A_PREV_CHUNKS, name="mixer_a")
        kb_col = (3 * A_WIDTH + B_Q_WIDTH) // B_KV_WIDTH
        sinks = (b_sinks[i].astype(F32) * np.float32(LOG2_E))[b_lane_heads]
        yb = _band_attention(qkv3, bias_b, sinks, q_col=3, k_col=kb_col, v_col=kb_col + 1,
                             n_heads=B_Q_HEADS, n_kv=B_KV_HEADS, n_prev=B_PREV_CHUNKS, name="mixer_b")

        h = _dense_call(
            _merge_ffn2_ple_kernel, n_rows,
            in_specs=[_row_tiled(D_MODEL), _row_tiled(A_WIDTH), _row_tiled(B_Q_WIDTH), _row_tiled(PLE_DIM),
                      _resident((1, D_MODEL)), _resident((D_MODEL, 2 * D_MODEL)), _resident((A_WIDTH, D_MODEL)),
                      _resident((B_Q_WIDTH, D_MODEL)), _resident((D_MODEL, D_MODEL)),
                      _resident((1, D_MODEL)), in_hbm, in_hbm,
                      _resident((1, D_MODEL)), _resident((D_MODEL, D_MODEL)), _resident((PLE_DIM, D_MODEL))],
            out_specs=_row_tiled(D_MODEL), out_shape=row_f32,
            scratch_shapes=[act_scratch] + _ffn_weight_scratch(), name="merge_ffn2_ple",
        )(h1, ya.reshape(n_rows, A_WIDTH), yb.reshape(n_rows, B_Q_WIDTH), p[i].reshape(n_rows, PLE_DIM),
          vec(mix_norm), w_gate[i].astype(BF16), w_proj_a[i].astype(BF16), w_proj_b_i,
          w_out[i].astype(BF16), vec(ffn2_norm), ffn2_w_gu[i].astype(F32), ffn2_w_down[i].astype(F32),
          vec(ple_norm), w_ple_gate[i].astype(BF16), w_ple_proj[i].astype(BF16))
    return h.reshape(b, s, D_MODEL)
```

```python
import functools

import numpy as np
import jax
import jax.numpy as jnp
from jax import lax
from jax.experimental import pallas as pl
from jax.experimental.pallas import tpu as pltpu

D_MODEL = 1024
D_FF = 2816
PLE_DIM = 256
CHUNK = 64
HEAD_DIM = 64
A_HEADS = 8
A_PREV_CHUNKS = 8
A_MAX_REL = 128
B_Q_HEADS = 8
B_KV_HEADS = 2
B_GROUP = B_Q_HEADS // B_KV_HEADS
B_PREV_CHUNKS = 2
A_WIDTH = A_HEADS * HEAD_DIM
B_Q_WIDTH = B_Q_HEADS * HEAD_DIM
B_KV_WIDTH = B_KV_HEADS * HEAD_DIM
IN_COLS = 3 * A_WIDTH + B_Q_WIDTH + 2 * B_KV_WIDTH
EPS = 1e-6
NEG_INF = -1e30
LOG2_E = 1.4426950408889634

V7X_MXU_DIM = 256
V7X_LANES = 128
V7X_VMEM_BYTES = 64 * 1024 * 1024
VMEM_LIMIT_BYTES = V7X_VMEM_BYTES - 8 * 1024 * 1024

ROW_TILE = 512
FF_CHUNK = V7X_MXU_DIM
UP_STAGE_ROWS = 64
DOWN_STAGE_ROWS = 128
Q_TILE = 2 * CHUNK
SCORE_LOOKAHEAD = 1

BF16 = jnp.bfloat16
F32 = jnp.float32


def _dot(a, b):
    return jnp.dot(a, b, preferred_element_type=F32)


def _sigmoid(x):
    return 1.0 / (1.0 + jnp.exp(-x))


def _rms_norm(h, gain):
    ms = jnp.mean(h * h, axis=-1, keepdims=True)
    return (h * lax.rsqrt(ms + EPS)) * gain


def _swiglu(xn, wgu_ref, wd_ref, act_ref):
    for j in range(D_FF // FF_CHUNK):
        lo = j * FF_CHUNK
        g = _dot(xn, wgu_ref[:, lo:lo + FF_CHUNK])
        u = _dot(xn, wgu_ref[:, D_FF + lo:D_FF + lo + FF_CHUNK])
        act_ref[:, lo:lo + FF_CHUNK] = ((g * _sigmoid(g)) * u).astype(BF16)
    return _dot(act_ref[...], wd_ref[...])


def _fetch_ffn_weights(wgu_hbm, wd_hbm, wgu_ref, wd_ref, stage_gu_ref, stage_d_ref, sem_ref):
    @pl.when(pl.program_id(0) == 0)
    def _():
        for w, (src, dst, stage) in enumerate(((wgu_hbm, wgu_ref, stage_gu_ref),
                                               (wd_hbm, wd_ref, stage_d_ref))):
            rows = stage.shape[1]
            n_chunks = src.shape[0] // rows

            def copy(c, src=src, stage=stage, rows=rows, w=w):
                return pltpu.make_async_copy(src.at[pl.ds(c * rows, rows)], stage.at[c % 2],
                                             sem_ref.at[w, c % 2])

            copy(0).start()
            for c in range(n_chunks):
                if c + 1 < n_chunks:
                    copy(c + 1).start()
                copy(c).wait()
                dst[c * rows:(c + 1) * rows, :] = stage[c % 2].astype(BF16)


def _ffn_weight_scratch():
    return [pltpu.VMEM((D_MODEL, 2 * D_FF), BF16), pltpu.VMEM((D_FF, D_MODEL), BF16),
            pltpu.VMEM((2, UP_STAGE_ROWS, 2 * D_FF), F32), pltpu.VMEM((2, DOWN_STAGE_ROWS, D_MODEL), F32),
            pltpu.SemaphoreType.DMA((2, 2))]


def _head_pair_norm(x, gain):
    first_head = lax.broadcasted_iota(jnp.int32, x.shape, 1) < HEAD_DIM
    sq = x * x
    s0 = jnp.sum(jnp.where(first_head, sq, 0.0), axis=-1, keepdims=True)
    s1 = jnp.sum(jnp.where(first_head, 0.0, sq), axis=-1, keepdims=True)
    ms = jnp.where(first_head, s0, s1) * (1.0 / HEAD_DIM)
    return (x * lax.rsqrt(ms + EPS)) * gain


_QKA_COLS = (0, 2 * A_WIDTH)
_VA_COLS = (2 * A_WIDTH, 3 * A_WIDTH)
_QB_COLS = (3 * A_WIDTH, 3 * A_WIDTH + B_Q_WIDTH)
_KB_COLS = (_QB_COLS[1], _QB_COLS[1] + B_KV_WIDTH)
_KVB_COLS = (_QB_COLS[1], IN_COLS)


def _ffn1_proj_kernel(x_ref, g1_ref, wgu_hbm, wd_hbm, gm_ref, win_ref, qkg_ref,
                      h1_ref, qkv_ref, act_ref, y_ref, wgu_ref, wd_ref, *fetch_scratch):
    _fetch_ffn_weights(wgu_hbm, wd_hbm, wgu_ref, wd_ref, *fetch_scratch)
    x = x_ref[...]
    h1 = x + 0.5 * _swiglu(_rms_norm(x, g1_ref[...]).astype(BF16), wgu_ref, wd_ref, act_ref)
    h1_ref[...] = h1
    u = _rms_norm(h1, gm_ref[...]).astype(BF16)
    for lo, hi in (_QKA_COLS, _QB_COLS):
        y_ref[:, lo:hi] = _dot(u, win_ref[:, lo:hi])
    half = ROW_TILE // 2
    lo, hi = _KVB_COLS
    y_ref[:half, lo:hi] = _dot(u[:half], win_ref[:, lo:hi])
    y_ref[half:, lo:hi] = _dot(u[half:], win_ref[:, lo:hi])
    lo, hi = _VA_COLS
    y_ref[:, lo:hi] = _dot(u, win_ref[:, lo:hi])

    for first, last in (_QKA_COLS, _QB_COLS, _KB_COLS):
        for lo in range(first, last, V7X_LANES):
            cols = slice(lo, lo + V7X_LANES)
            qkv_ref[:, cols] = _head_pair_norm(y_ref[:, cols], qkg_ref[:, cols]).astype(BF16)
    for lo, hi in (_VA_COLS, (_KB_COLS[1], IN_COLS)):
        qkv_ref[:, lo:hi] = y_ref[:, lo:hi].astype(BF16)


def _merge_ffn2_ple_kernel(h1_ref, ya_ref, yb_ref, p_ref, gm_ref, wg_ref, wa_ref, wb_ref, wo_ref,
                           g2_ref, wgu_hbm, wd_hbm, gp_ref, wpg_ref, wpe_ref, out_ref, act_ref,
                           wgu_ref, wd_ref, *fetch_scratch):
    _fetch_ffn_weights(wgu_hbm, wd_hbm, wgu_ref, wd_ref, *fetch_scratch)
    h1 = h1_ref[...]
    u = _rms_norm(h1, gm_ref[...]).astype(BF16)
    ga = _sigmoid(_dot(u, wg_ref[:, :D_MODEL]))
    gb = _sigmoid(_dot(u, wg_ref[:, D_MODEL:]))
    merged = ga * _dot(ya_ref[...], wa_ref[...]) + gb * _dot(yb_ref[...], wb_ref[...])
    h2 = h1 + _dot(merged.astype(BF16), wo_ref[...])
    h3 = h2 + 0.5 * _swiglu(_rms_norm(h2, g2_ref[...]).astype(BF16), wgu_ref, wd_ref, act_ref)
    gate = _sigmoid(_dot(_rms_norm(h3, gp_ref[...]).astype(BF16), wpg_ref[...]))
    out_ref[...] = h3 + gate * _dot(p_ref[...].astype(BF16), wpe_ref[...])


def _band_attention_kernel(*refs, n_pairs, kv_pairs, pad, has_sink):
    if has_sink:
        sink_ref, q_ref, k_ref, v_ref, bias_ref, o_ref = refs
    else:
        q_ref, k_ref, v_ref, bias_ref, o_ref = refs
    seq = k_ref.shape[1]
    pair = 2 * HEAD_DIM
    first_head = lax.broadcasted_iota(jnp.int32, (Q_TILE, pair), 1) < HEAD_DIM
    first_rows = lax.broadcasted_iota(jnp.int32, (2 * Q_TILE, 1), 0) < Q_TILE
    ones = jnp.ones((Q_TILE + pad, pair), BF16)

    def key_rows(tile):
        return max(tile * Q_TILE - pad, 0), (tile + 1) * Q_TILE

    def scores(tile, pp):
        kp = pp % kv_pairs
        lo, hi = key_rows(tile)
        q = q_ref[0, tile * Q_TILE:(tile + 1) * Q_TILE, pair * pp:pair * (pp + 1)].astype(F32)
        q2 = jnp.concatenate([jnp.where(first_head, q, 0.0), jnp.where(first_head, 0.0, q)], axis=0)
        k = k_ref[0, lo:hi, pair * kp:pair * (kp + 1)]
        s = lax.dot_general(q2.astype(BF16), k, (((1,), (1,)), ((), ())), preferred_element_type=F32)
        clipped = Q_TILE + pad - (hi - lo)
        return s + bias_ref[2 * pp:2 * pp + 2, :, clipped:].reshape(2 * Q_TILE, hi - lo)

    def finish(tile, pp, s):
        kp = pp % kv_pairs
        lo, hi = key_rows(tile)
        v = jnp.concatenate([v_ref[0, lo:hi, pair * kp:pair * (kp + 1)], ones[:hi - lo]], axis=1)
        m = jnp.max(s, axis=-1, keepdims=True)
        if has_sink:
            sink = jnp.where(first_rows, sink_ref[2 * pp], sink_ref[2 * pp + 1])
            m = jnp.maximum(m, sink)
        o2 = _dot(jnp.exp2(s - m).astype(BF16), v)
        denom = o2[:, pair:]
        if has_sink:
            denom = denom + jnp.exp2(sink - m)
        o2 = o2[:, :pair] / denom
        o_ref[0, tile * Q_TILE:(tile + 1) * Q_TILE, pair * pp:pair * (pp + 1)] = (
            jnp.where(first_head, o2[:Q_TILE], o2[Q_TILE:]).astype(BF16))

    units = [(tile, pp) for tile in range(seq // Q_TILE) for pp in range(n_pairs)]
    pending = [scores(*u) for u in units[:SCORE_LOOKAHEAD]]
    for n, u in enumerate(units):
        if n + SCORE_LOOKAHEAD < len(units):
            pending.append(scores(*units[n + SCORE_LOOKAHEAD]))
        finish(*u, pending.pop(0))


def _resident(shape):
    return pl.BlockSpec(shape, lambda *_: (0,) * len(shape), pipeline_mode=pl.Buffered(1))


def _row_tiled(width):
    return pl.BlockSpec((ROW_TILE, width), lambda i: (i, 0))


def _dense_call(kernel, n_rows, in_specs, out_specs, out_shape, scratch_shapes, name):
    return pl.pallas_call(
        kernel,
        grid=(n_rows // ROW_TILE,),
        in_specs=in_specs,
        out_specs=out_specs,
        out_shape=out_shape,
        scratch_shapes=scratch_shapes,
        compiler_params=pltpu.CompilerParams(
            dimension_semantics=("arbitrary",), vmem_limit_bytes=VMEM_LIMIT_BYTES),
        name=name,
    )


def _band_attention(qkv, bias, sinks, *, q_col, k_col, v_col, n_heads, n_kv, n_prev, name):
    b, s, _ = qkv.shape
    pad = n_prev * CHUNK
    qw, kw = n_heads * HEAD_DIM, n_kv * HEAD_DIM
    has_sink = sinks is not None
    in_specs = [
        pl.BlockSpec((1, s, qw), lambda bi: (bi, 0, q_col)),
        pl.BlockSpec((1, s, kw), lambda bi: (bi, 0, k_col)),
        pl.BlockSpec((1, s, kw), lambda bi: (bi, 0, v_col)),
        _resident(bias.shape),
    ]
    args = [qkv, qkv, qkv, bias]
    if has_sink:
        in_specs.insert(0, pl.BlockSpec(memory_space=pltpu.SMEM))
        args.insert(0, sinks)
    return pl.pallas_call(
        functools.partial(_band_attention_kernel, n_pairs=n_heads // 2, kv_pairs=n_kv // 2,
                          pad=pad, has_sink=has_sink),
        grid=(b,),
        in_specs=in_specs,
        out_specs=pl.BlockSpec((1, s, qw), lambda bi: (bi, 0, 0)),
        out_shape=jax.ShapeDtypeStruct((b, s, qw), BF16),
        compiler_params=pltpu.CompilerParams(
            dimension_semantics=("arbitrary",), vmem_limit_bytes=VMEM_LIMIT_BYTES),
        name=name,
    )(*args)


def _band_geometry(n_prev):
    pad = n_prev * CHUNK
    assert pad % Q_TILE == 0
    i = np.arange(Q_TILE)[:, None]
    j = np.arange(Q_TILE + pad)[None, :]
    dist = i + pad - j
    q_chunk, k_chunk = i // CHUNK, j // CHUNK
    in_band = (k_chunk >= q_chunk) & (k_chunk <= q_chunk + n_prev)
    return dist, in_band


def _take_runs(table, idx):
    pieces, i = [], 0
    while i < len(idx):
        j = i
        while j + 1 < len(idx) and idx[j + 1] == idx[j] + 1:
            j += 1
        if j == i:
            while j + 1 < len(idx) and idx[j + 1] == idx[i]:
                j += 1
            pieces.append(jnp.broadcast_to(table[:, idx[i]:idx[i] + 1], (table.shape[0], j - i + 1)))
        else:
            pieces.append(table[:, idx[i]:idx[j] + 1])
        i = j + 1
    return jnp.concatenate(pieces, axis=1)


def _toeplitz(g, n_rows, n_cols):
    length = n_rows + n_cols - 1
    skew = jnp.tile(g, (1, n_rows + 1))[:, :n_rows * (length + 1)].reshape(-1, n_rows, length + 1)
    return skew[:, :, :n_cols][:, :, ::-1]


def _alibi_slopes(n_heads):
    return np.array([2.0 ** (-8.0 * (h + 1) / n_heads) for h in range(n_heads)], dtype=np.float32)


def kernel(x, p, ffn1_norm, ffn1_w_gu, ffn1_w_down, mix_norm, w_in, a_q_norm, a_k_norm, a_rel_bias, b_q_norm, b_k_norm, b_sinks, w_gate, w_proj_a, w_proj_b, w_out, ffn2_norm, ffn2_w_gu, ffn2_w_down, ple_norm, w_ple_gate, w_ple_proj):
    b, s, d = x.shape
    depth = ffn1_w_gu.shape[0]
    assert d == D_MODEL and s % Q_TILE == 0 and (b * s) % ROW_TILE == 0
    n_rows = b * s
    q_scale = np.float32(HEAD_DIM ** -0.5 * LOG2_E)

    win_a = Q_TILE + A_PREV_CHUNKS * CHUNK
    _, band_a = _band_geometry(A_PREV_CHUNKS)
    diag = np.arange(Q_TILE + win_a - 1) - (win_a - 1) + A_PREV_CHUNKS * CHUNK
    rel_idx = np.clip(diag, -A_MAX_REL, A_MAX_REL) + A_MAX_REL
    dist_b, band_b = _band_geometry(B_PREV_CHUNKS)
    b_lane_heads = np.arange(B_Q_HEADS).reshape(B_KV_HEADS, B_GROUP).T.reshape(-1)
    alibi = -_alibi_slopes(B_Q_HEADS)[:, None, None] * np.abs(dist_b).astype(np.float32)[None]
    bias_b = jnp.asarray(np.where(band_b[None], (alibi * np.float32(LOG2_E))[b_lane_heads],
                                  np.float32(NEG_INF)))

    row_f32 = jax.ShapeDtypeStruct((n_rows, D_MODEL), F32)
    in_hbm = pl.BlockSpec(memory_space=pl.ANY)
    act_scratch = pltpu.VMEM((ROW_TILE, D_FF), BF16)

    h = x.reshape(n_rows, D_MODEL)
    for i in range(depth):
        vec = lambda g: g[i].astype(F32).reshape(1, -1)
        qk_gain = jnp.concatenate([
            jnp.tile(a_q_norm[i], A_HEADS) * q_scale, jnp.tile(a_k_norm[i], A_HEADS),
            jnp.ones((A_WIDTH,), F32),
            jnp.tile(b_q_norm[i], B_Q_HEADS) * q_scale, jnp.tile(b_k_norm[i], B_KV_HEADS),
            jnp.ones((B_KV_WIDTH,), F32)]).astype(F32).reshape(1, IN_COLS)
        qb_lo = 3 * A_WIDTH
        w_in_bf = w_in[i].astype(BF16)
        w_qb = w_in_bf[:, qb_lo:qb_lo + B_Q_WIDTH].reshape(D_MODEL, B_KV_HEADS, B_GROUP, HEAD_DIM)
        w_in_i = jnp.concatenate([w_in_bf[:, :qb_lo], w_qb.transpose(0, 2, 1, 3).reshape(D_MODEL, B_Q_WIDTH),
                                  w_in_bf[:, qb_lo + B_Q_WIDTH:]], axis=1)
        w_proj_b_i = (w_proj_b[i].astype(BF16).reshape(B_KV_HEADS, B_GROUP, HEAD_DIM, D_MODEL)
                      .transpose(1, 0, 2, 3).reshape(B_Q_WIDTH, D_MODEL))

        h1, qkv = _dense_call(
            _ffn1_proj_kernel, n_rows,
            in_specs=[_row_tiled(D_MODEL), _resident((1, D_MODEL)), in_hbm, in_hbm,
                      _resident((1, D_MODEL)), _resident((D_MODEL, IN_COLS)), _resident((1, IN_COLS))],
            out_specs=[_row_tiled(D_MODEL), _row_tiled(IN_COLS)],
            out_shape=[row_f32, jax.ShapeDtypeStruct((n_rows, IN_COLS), BF16)],
            scratch_shapes=[act_scratch, pltpu.VMEM((ROW_TILE, IN_COLS), F32)] + _ffn_weight_scratch(),
            name="ffn1_proj",
        )(h, vec(ffn1_norm), ffn1_w_gu[i].astype(F32), ffn1_w_down[i].astype(F32),
          vec(mix_norm), w_in_i, qk_gain)

        qkv3 = qkv.reshape(b, s, IN_COLS)
        rel_diag = _take_runs(a_rel_bias[i].astype(F32) * np.float32(LOG2_E), rel_idx)
        rel = _toeplitz(rel_diag, Q_TILE, win_a)
        bias_a = jnp.where(band_a[None], rel, NEG_INF)
        ya = _band_attention(qkv3, bias_a, None, q_col=0, k_col=1, v_col=2,
                             n_heads=A_HEADS, n_kv=A_HEADS, n_prev=A_PREV_CHUNKS, name="mixer_a")
        kb_col = (3 * A_WIDTH + B_Q_WIDTH) // B_KV_WIDTH
        sinks = (b_sinks[i].astype(F32) * np.float32(LOG2_E))[b_lane_heads]
        yb = _band_attention(qkv3, bias_b, sinks, q_col=3, k_col=kb_col, v_col=kb_col + 1,
                             n_heads=B_Q_HEADS, n_kv=B_KV_HEADS, n_prev=B_PREV_CHUNKS, name="mixer_b")

        h = _dense_call(
            _merge_ffn2_ple_kernel, n_rows,
            in_specs=[_row_tiled(D_MODEL), _row_tiled(A_WIDTH), _row_tiled(B_Q_WIDTH), _row_tiled(PLE_DIM),
                      _resident((1, D_MODEL)), _resident((D_MODEL, 2 * D_MODEL)), _resident((A_WIDTH, D_MODEL)),
                      _resident((B_Q_WIDTH, D_MODEL)), _resident((D_MODEL, D_MODEL)),
                      _resident((1, D_MODEL)), in_hbm, in_hbm,
                      _resident((1, D_MODEL)), _resident((D_MODEL, D_MODEL)), _resident((PLE_DIM, D_MODEL))],
            out_specs=_row_tiled(D_MODEL), out_shape=row_f32,
            scratch_shapes=[act_scratch] + _ffn_weight_scratch(), name="merge_ffn2_ple",
        )(h1, ya.reshape(n_rows, A_WIDTH), yb.reshape(n_rows, B_Q_WIDTH), p[i].reshape(n_rows, PLE_DIM),
          vec(mix_norm), w_gate[i].astype(BF16), w_proj_a[i].astype(BF16), w_proj_b_i,
          w_out[i].astype(BF16), vec(ffn2_norm), ffn2_w_gu[i].astype(F32), ffn2_w_down[i].astype(F32),
          vec(ple_norm), w_ple_gate[i].astype(BF16), w_ple_proj[i].astype(BF16))
    return h.reshape(b, s, D_MODEL)
```
